```python
import jax
import jax.numpy as jnp
from jax import lax
import numpy as np

D_MODEL = 1024
BATCH = 8
SEQ = 4096
DEPTH = 1

CHUNK = 64
Q_BLOCK = 128
N_MEM = 256
EPS = 1e-6
SB_HD = 128
SB_HEADS = D_MODEL // SB_HD
SB_W = SB_HEADS * SB_HD
ML_HEADS = 4
ML_HD = D_MODEL // ML_HEADS
ML_W = ML_HEADS * ML_HD
X_HEADS = 4
X_HD = D_MODEL // X_HEADS
X_W = X_HEADS * X_HD
CONV_W = 4
D_FF = 4 * D_MODEL
N_BRANCH = 3
IN_SIZES = (SB_W, SB_W, SB_W, ML_W, ML_W, ML_W, ML_W, ML_HEADS, ML_HEADS, X_W, N_BRANCH * D_MODEL)
IN_SPLITS = tuple(int(v) for v in np.cumsum(IN_SIZES)[:-1])
N_IN = int(sum(IN_SIZES))

kernel_name = "hybrid_sb_mlstm_xattn_block"


def _rmsnorm(x, g):
    xf = x.astype(jnp.float32)
    y = xf * lax.rsqrt(jnp.mean(xf * xf, axis=-1, keepdims=True) + EPS)
    return (y * g.astype(jnp.float32)).astype(x.dtype)


def _split_heads(t, n):
    b, s, _ = t.shape
    return t.reshape(b, s, n, -1).transpose(0, 2, 1, 3)


def _merge_heads(t):
    b, h, s, d = t.shape
    return t.transpose(0, 2, 1, 3).reshape(b, s, h * d)


def _causal_conv(u, w, b):
    s = u.shape[1]
    up = jnp.pad(u, ((0, 0), (CONV_W - 1, 0), (0, 0)))
    return sum((up[:, j:j + s, :] * w[j] for j in range(CONV_W)), b)


def _stick_breaking(q, k, v):
    _, _, s_len, dh = q.shape
    scale = dh ** -0.5
    kf = k.astype(jnp.float32)
    vf = v.astype(jnp.float32)
    outs = []
    for blk in range(s_len // Q_BLOCK):
        start, end = blk * Q_BLOCK, (blk + 1) * Q_BLOCK
        qb = q[:, :, start:end].astype(jnp.float32)
        z = jnp.einsum('bhqd,bhkd->bhqk', qb, kf[:, :, :end]) * scale
        q_pos = start + jnp.arange(Q_BLOCK)
        k_pos = jnp.arange(end)
        causal = k_pos[None, :] < q_pos[:, None]
        log_1m = jnp.where(causal, jax.nn.log_sigmoid(-z), 0.0)
        later = lax.cumsum(log_1m, axis=3, reverse=True) - log_1m
        log_a = jnp.where(causal, jax.nn.log_sigmoid(z) + later, -jnp.inf)
        outs.append(jnp.einsum('bhqk,bhkd->bhqd', jnp.exp(log_a), vf[:, :, :end]))
    return jnp.concatenate(outs, axis=2)


def _mlstm(q, k, v, i_pre, f_pre):
    f32 = jnp.float32
    b, h, s_len, dh = q.shape
    nc = s_len // CHUNK
    q = q.astype(f32)
    k = k.astype(f32) * (dh ** -0.5)
    v = v.astype(f32)
    i_pre = i_pre.astype(f32)
    log_f = jax.nn.log_sigmoid(f_pre.astype(f32))

    def chunks(t):
        return jnp.moveaxis(t.reshape(b, h, nc, CHUNK, *t.shape[3:]), 2, 0)

    tri = jnp.tril(jnp.ones((CHUNK, CHUNK), dtype=bool))

    def step(carry, xs):
        c_prev, n_prev, m_prev = carry
        qc, kc, vc, ic, lfc = xs
        bcum = jnp.cumsum(lfc, axis=-1)
        d = jnp.where(tri, bcum[..., :, None] - bcum[..., None, :] + ic[..., None, :], -jnp.inf)
        m_inter = bcum + m_prev[..., None]
        m_t = jnp.maximum(m_inter, jnp.max(d, axis=-1))
        w = jnp.exp(d - m_t[..., None])
        s_inter = jnp.exp(m_inter - m_t)
        sc = jnp.einsum('bhtd,bhsd->bhts', qc, kc) * w
        num = jnp.einsum('bhts,bhsd->bhtd', sc, vc) + s_inter[..., None] * jnp.einsum('bhvk,bhtk->bhtv', c_prev, qc)
        den = jnp.sum(sc, axis=-1) + s_inter * jnp.einsum('bhtk,bhk->bht', qc, n_prev)
        h_c = num / jnp.maximum(jnp.abs(den), jnp.exp(-m_t))[..., None]
        b_end = bcum[..., -1]
        g = b_end[..., None] - bcum + ic
        m_new = jnp.maximum(b_end + m_prev, jnp.max(g, axis=-1))
        decay = jnp.exp(b_end + m_prev - m_new)
        wk = jnp.exp(g - m_new[..., None])
        c_new = decay[..., None, None] * c_prev + jnp.einsum('bhsv,bhsk->bhvk', vc * wk[..., None], kc)
        n_new = decay[..., None] * n_prev + jnp.einsum('bhs,bhsk->bhk', wk, kc)
        return (c_new, n_new, m_new), h_c

    init = (jnp.zeros((b, h, dh, dh), f32), jnp.zeros((b, h, dh), f32), jnp.zeros((b, h), f32))
    _, hs = lax.scan(step, init, (chunks(q), chunks(k), chunks(v), chunks(i_pre), chunks(log_f)))
    return jnp.moveaxis(hs, 0, 2).reshape(b, h, s_len, dh)


def _cross_attention(q, mk, mv, gq, gk):
    dh = q.shape[-1]
    qn = _rmsnorm(q, gq)
    kn = _rmsnorm(mk, gk)
    logits = jnp.einsum('bhsd,bhmd->bhsm', qn, kn).astype(jnp.float32) * (dh ** -0.5)
    p = jax.nn.softmax(logits, axis=-1)
    return jnp.einsum('bhsm,bhmd->bhsd', p.astype(mv.dtype), mv)


def setup_inputs(seed: int = 0) -> dict:
    key = jax.random.key(seed)
    ks = jax.random.split(key, 24)
    f32 = jnp.float32
    L = DEPTH

    def nrm(k, shape, scale):
        return jax.random.normal(k, shape, f32) * scale

    def gain(k, shape):
        return 1.0 + 0.02 * jax.random.normal(k, shape, f32)

    b_i = nrm(ks[4], (L, ML_HEADS), 0.1)
    b_f = jnp.linspace(3.0, 6.0, ML_HEADS, dtype=f32)[None, :] + nrm(ks[5], (L, ML_HEADS), 0.1)
    return {
        "x": nrm(ks[0], (BATCH, SEQ, D_MODEL), 1.0),
        "mem": nrm(ks[1], (BATCH, N_MEM, D_MODEL), 1.0),
        "g_mix": gain(ks[2], (L, D_MODEL)),
        "w_in": nrm(ks[3], (L, D_MODEL, N_IN), D_MODEL ** -0.5),
        "b_if": jnp.concatenate([b_i, b_f], axis=-1),
        "b_gate": nrm(ks[6], (L, N_BRANCH * D_MODEL), 0.02),
        "conv_w": nrm(ks[7], (L, CONV_W, 2 * ML_W), CONV_W ** -0.5),
        "conv_b": nrm(ks[8], (L, 2 * ML_W), 0.02),
        "ml_norm_g": gain(ks[9], (L, ML_W)),
        "g_mem": gain(ks[10], (L, D_MODEL)),
        "w_mem_kv": nrm(ks[11], (L, D_MODEL, 2 * X_W), D_MODEL ** -0.5),
        "q_norm_g": gain(ks[12], (L, X_HD)),
        "k_norm_g": gain(ks[13], (L, X_HD)),
        "w_sb_proj": nrm(ks[14], (L, SB_W, D_MODEL), SB_W ** -0.5),
        "w_ml_proj": nrm(ks[15], (L, ML_W, D_MODEL), ML_W ** -0.5),
        "w_x_proj": nrm(ks[16], (L, X_W, D_MODEL), X_W ** -0.5),
        "w_out": nrm(ks[17], (L, D_MODEL, D_MODEL), D_MODEL ** -0.5),
        "g_mlp": gain(ks[18], (L, D_MODEL)),
        "w_ff1": nrm(ks[19], (L, D_MODEL, D_FF), D_MODEL ** -0.5),
        "w_ff2": nrm(ks[20], (L, D_FF, D_MODEL), D_FF ** -0.5),
    }


def reference(x, mem, g_mix, w_in, b_if, b_gate, conv_w, conv_b, ml_norm_g, g_mem, w_mem_kv,
              q_norm_g, k_norm_g, w_sb_proj, w_ml_proj, w_x_proj, w_out, g_mlp, w_ff1, w_ff2):
    dt = x.dtype
    bsz, seq, _ = x.shape
    for l in range(DEPTH):
        h = _rmsnorm(x, g_mix[l])
        z = h @ w_in[l]
        (sb_q, sb_k, sb_v, ml_q, ml_k, ml_v, ml_o, ml_i, ml_f, x_q, gate_pre) = jnp.split(z, IN_SPLITS, axis=-1)

        y_sb = _merge_heads(_stick_breaking(_split_heads(sb_q, SB_HEADS), _split_heads(sb_k, SB_HEADS),
                                            _split_heads(sb_v, SB_HEADS))).astype(dt)

        qk = jax.nn.silu(_causal_conv(jnp.concatenate([ml_q, ml_k], axis=-1), conv_w[l], conv_b[l]))
        mq, mk = jnp.split(qk, 2, axis=-1)
        i_pre = (ml_i + b_if[l, :ML_HEADS]).transpose(0, 2, 1)
        f_pre = (ml_f + b_if[l, ML_HEADS:]).transpose(0, 2, 1)
        hm = _mlstm(_split_heads(mq, ML_HEADS), _split_heads(mk, ML_HEADS), _split_heads(ml_v, ML_HEADS), i_pre, f_pre)
        hm = _rmsnorm(hm, ml_norm_g[l].reshape(ML_HEADS, 1, ML_HD))
        y_ml = (_merge_heads(hm) * jax.nn.sigmoid(ml_o.astype(jnp.float32))).astype(dt)

        kv = _rmsnorm(mem, g_mem[l]) @ w_mem_kv[l]
        mem_k, mem_v = jnp.split(kv, 2, axis=-1)
        y_x = _merge_heads(_cross_attention(_split_heads(x_q, X_HEADS), _split_heads(mem_k, X_HEADS),
                                            _split_heads(mem_v, X_HEADS), q_norm_g[l], k_norm_g[l])).astype(dt)

        gates = jax.nn.sigmoid(gate_pre + b_gate[l]).reshape(bsz, seq, N_BRANCH, D_MODEL)
        mixed = (gates[:, :, 0] * (y_sb @ w_sb_proj[l])
                 + gates[:, :, 1] * (y_ml @ w_ml_proj[l])
                 + gates[:, :, 2] * (y_x @ w_x_proj[l]))
        x = x + mixed @ w_out[l]

        u = _rmsnorm(x, g_mlp[l]) @ w_ff1[l]
        x = x + jnp.square(jax.nn.relu(u)) @ w_ff2[l]
    return x
```

```python
import functools

import jax
import jax.numpy as jnp
from jax import lax
from jax.experimental import pallas as pl
from jax.experimental.pallas import tpu as pltpu

EPS = 1e-6
SB_HD = 128
ML_HEADS = 4
X_HEADS = 4
CONV_W = 4
N_BRANCH = 3
SB_BLOCK = 128
ML_CHUNK = 256
CONV_HALO = 8
VMEM_LIMIT = 56 * 1024 * 1024

F32 = jnp.float32
BF16 = jnp.bfloat16


def _params(*sem):
    return pltpu.CompilerParams(dimension_semantics=sem, vmem_limit_bytes=VMEM_LIMIT)


def _nt_dot(a, b):
    return lax.dot_general(a, b, (((1,), (1,)), ((), ())), preferred_element_type=F32)


def _tn_dot(a, b):
    return lax.dot_general(a, b, (((0,), (0,)), ((), ())), preferred_element_type=F32)


def _dot(a, b):
    return jnp.dot(a, b, preferred_element_type=F32)


def _rms(x, g):
    return x * lax.rsqrt(jnp.mean(x * x, axis=-1, keepdims=True) + EPS) * g


def _norm_kernel(x_ref, g_ref, o_ref):
    o_ref[...] = _rms(x_ref[...], g_ref[...]).astype(o_ref.dtype)


def _norm_cast(x2d, g, tm):
    t, d = x2d.shape
    return pl.pallas_call(
        _norm_kernel,
        grid=(t // tm,),
        in_specs=[pl.BlockSpec((tm, d), lambda i: (i, 0)), pl.BlockSpec((1, d), lambda i: (0, 0))],
        out_specs=pl.BlockSpec((tm, d), lambda i: (i, 0)),
        out_shape=jax.ShapeDtypeStruct((t, d), BF16),
        compiler_params=_params("parallel"),
        name="norm_cast",
    )(x2d, g.reshape(1, d))


def _matmul_kernel(a_ref, w_ref, o_ref):
    o_ref[...] = _dot(a_ref[...], w_ref[...]).astype(o_ref.dtype)


def _matmul(a, w, out_dtype, tm, tn, name):
    t, k = a.shape
    n = w.shape[1]
    return pl.pallas_call(
        _matmul_kernel,
        grid=(n // tn, t // tm),
        in_specs=[pl.BlockSpec((tm, k), lambda j, i: (i, 0)), pl.BlockSpec((k, tn), lambda j, i: (0, j))],
        out_specs=pl.BlockSpec((tm, tn), lambda j, i: (i, j)),
        out_shape=jax.ShapeDtypeStruct((t, n), out_dtype),
        compiler_params=_params("parallel", "parallel"),
        name=name,
    )(a, w)


def _sb_kernel(q_ref, k_ref, v_ref, tri_ref, o_ref, *, scale):
    blk = SB_BLOCK
    qi = pl.program_id(2)
    q = q_ref[0]
    tri = tri_ref[...]

    def tile(kj, carry, acc, diag):
        start = pl.multiple_of(kj * blk, blk)
        kb = k_ref[0, pl.ds(start, blk), :]
        vb = v_ref[0, pl.ds(start, blk), :]
        z = _nt_dot(q, kb) * scale
        sp = jnp.maximum(z, 0.0) + jnp.log(1.0 + jnp.exp(-jnp.abs(z)))
        if diag:
            causal = (lax.broadcasted_iota(jnp.int32, (blk, blk), 1)
                      < lax.broadcasted_iota(jnp.int32, (blk, blk), 0))
            sp = jnp.where(causal, sp, 0.0)
        hi = sp.astype(BF16)
        lo = (sp - hi.astype(F32)).astype(BF16)
        r = _dot(hi, tri) + _dot(lo, tri)
        later = r[:, :blk]
        total = r[:, blk:]
        a = jnp.exp(z - sp - later - carry)
        if diag:
            a = jnp.where(causal, a, 0.0)
        acc = acc + _dot(a.astype(BF16), vb)
        return carry + total, acc

    zeros = jnp.zeros((blk, SB_HD), F32)
    carry, acc = tile(qi, zeros, zeros, True)

    def body(i, c):
        return tile(qi - 1 - i, c[0], c[1], False)

    carry, acc = lax.fori_loop(0, qi, body, (carry, acc))
    o_ref[0] = acc.astype(o_ref.dtype)


def _sb_attention(qkv, n_heads):
    b, s, _ = qkv.shape
    blk = SB_BLOCK
    row = lax.broadcasted_iota(jnp.int32, (blk, 2 * blk), 0)
    col = lax.broadcasted_iota(jnp.int32, (blk, 2 * blk), 1)
    tri = jnp.where((col >= blk) | (row > col), 1.0, 0.0).astype(BF16)
    return pl.pallas_call(
        functools.partial(_sb_kernel, scale=SB_HD ** -0.5),
        grid=(b, n_heads, s // blk),
        in_specs=[
            pl.BlockSpec((1, blk, SB_HD), lambda bi, h, i: (bi, i, h)),
            pl.BlockSpec((1, s, SB_HD), lambda bi, h, i: (bi, 0, n_heads + h)),
            pl.BlockSpec((1, s, SB_HD), lambda bi, h, i: (bi, 0, 2 * n_heads + h)),
            pl.BlockSpec((blk, 2 * blk), lambda bi, h, i: (0, 0)),
        ],
        out_specs=pl.BlockSpec((1, blk, SB_HD), lambda bi, h, i: (bi, i, h)),
        out_shape=jax.ShapeDtypeStruct((b, s, n_heads * SB_HD), BF16),
        compiler_params=_params("parallel", "parallel", "arbitrary"),
        name="sb_attention",
    )(qkv, qkv, qkv, tri)


def _mlstm_kernel(q_ref, k_ref, v_ref, o_ref, gcol_ref, grow_ref, wq_ref, wk_ref, bq_ref, bk_ref, ng_ref,
                  y_ref, qext, kext, ct_ref, n_ref, m_ref, *, k_scale):
    L = ML_CHUNK
    H = CONV_HALO
    c = pl.program_id(2)

    @pl.when(c == 0)
    def _():
        qext[0:H, :] = jnp.zeros((H, qext.shape[1]), F32)
        kext[0:H, :] = jnp.zeros((H, kext.shape[1]), F32)
        ct_ref[...] = jnp.zeros_like(ct_ref)
        n_ref[...] = jnp.zeros_like(n_ref)
        m_ref[...] = jnp.zeros_like(m_ref)

    def conv_silu(raw_ref, ext, w_ref, b_ref):
        raw = raw_ref[0]
        ext[H:H + L, :] = raw
        u = b_ref[...] + w_ref[CONV_W - 1:CONV_W, :] * raw
        for j in range(CONV_W - 1):
            shift = CONV_W - 1 - j
            u = u + w_ref[j:j + 1, :] * ext[H - shift:H - shift + L, :]
        ext[0:H, :] = raw[L - H:L, :]
        return u * jax.nn.sigmoid(u)

    q = conv_silu(q_ref, qext, wq_ref, bq_ref)
    k = conv_silu(k_ref, kext, wk_ref, bk_ref) * k_scale
    v = v_ref[0]
    qb = q.astype(BF16)
    kb = k.astype(BF16)

    i_col = gcol_ref[0, 0, :, 0:1]
    lf_col = jax.nn.log_sigmoid(gcol_ref[0, 0, :, 1:2])
    i_row = grow_ref[0, 0, 0:1, :]
    lf_row = jax.nn.log_sigmoid(grow_ref[0, 0, 1:2, :])

    t_idx = lax.broadcasted_iota(jnp.int32, (L, L), 0)
    s_idx = lax.broadcasted_iota(jnp.int32, (L, L), 1)
    tril = s_idx <= t_idx
    bcum_col = jnp.sum(jnp.where(tril, lf_row, 0.0), axis=1, keepdims=True)
    bcum_row = jnp.sum(jnp.where(t_idx <= s_idx, lf_col, 0.0), axis=0, keepdims=True)
    m_prev = m_ref[...]
    d = jnp.where(tril, bcum_col - bcum_row + i_row, -jnp.inf)
    m_inter = bcum_col + m_prev
    m_t = jnp.maximum(m_inter, jnp.max(d, axis=1, keepdims=True))
    w = jnp.exp(d - m_t)
    s_inter = jnp.exp(m_inter - m_t)

    ct = ct_ref[...]
    n_prev = n_ref[...]
    sc = _nt_dot(qb, kb) * w
    num = _dot(sc.astype(BF16), v) + s_inter * _dot(qb, ct.astype(BF16))
    den = jnp.sum(sc, axis=1, keepdims=True) + s_inter * jnp.sum(q * n_prev, axis=1, keepdims=True)
    h = num / jnp.maximum(jnp.abs(den), jnp.exp(-m_t))

    b_end = jnp.sum(lf_row, axis=1, keepdims=True)
    g_col = b_end - bcum_col + i_col
    m_new = jnp.maximum(b_end + m_prev, jnp.max(g_col, axis=0, keepdims=True))
    decay = jnp.exp(b_end + m_prev - m_new)
    wk = jnp.exp(g_col - m_new)
    vw = (v.astype(F32) * wk).astype(BF16)
    ct_ref[...] = decay * ct + _tn_dot(kb, vw)
    n_ref[...] = decay * n_prev + jnp.sum(k * wk, axis=0, keepdims=True)
    m_ref[...] = m_new

    hn = _rms(h, ng_ref[...])
    y_ref[0] = (hn * jax.nn.sigmoid(o_ref[0])).astype(y_ref.dtype)


def _mlstm(mlqk, mlv, mlo, gcol, grow, conv_w, conv_b, norm_g):
    b, s, w2 = mlqk.shape
    wd = w2 // 2
    nh = ML_HEADS
    dh = wd // nh
    L = ML_CHUNK
    blk = lambda off: pl.BlockSpec((1, L, dh), lambda bi, h, c: (bi, c, off + h))
    return pl.pallas_call(
        functools.partial(_mlstm_kernel, k_scale=dh ** -0.5),
        grid=(b, nh, s // L),
        in_specs=[
            blk(0), blk(nh), blk(0), blk(0),
            pl.BlockSpec((1, 1, L, 2), lambda bi, h, c: (bi, h, c, 0)),
            pl.BlockSpec((1, 1, 2, L), lambda bi, h, c: (bi, h, 0, c)),
            pl.BlockSpec((CONV_W, dh), lambda bi, h, c: (0, h)),
            pl.BlockSpec((CONV_W, dh), lambda bi, h, c: (0, nh + h)),
            pl.BlockSpec((1, dh), lambda bi, h, c: (0, h)),
            pl.BlockSpec((1, dh), lambda bi, h, c: (0, nh + h)),
            pl.BlockSpec((1, dh), lambda bi, h, c: (0, h)),
        ],
        out_specs=blk(0),
        out_shape=jax.ShapeDtypeStruct((b, s, wd), BF16),
        scratch_shapes=[
            pltpu.VMEM((L + CONV_HALO, dh), F32),
            pltpu.VMEM((L + CONV_HALO, dh), F32),
            pltpu.VMEM((dh, dh), F32),
            pltpu.VMEM((1, dh), F32),
            pltpu.VMEM((1, 1), F32),
        ],
        compiler_params=_params("parallel", "parallel", "arbitrary"),
        name="mlstm",
    )(mlqk, mlqk, mlv, mlo, gcol, grow, conv_w, conv_w, conv_b.reshape(1, w2), conv_b.reshape(1, w2),
      norm_g.reshape(1, wd))


def _kv_prep_kernel(kv_ref, gk_ref, kn_ref, mv_ref):
    xw = kn_ref.shape[1]
    dh = xw // X_HEADS
    for h in range(X_HEADS):
        kh = kv_ref[:, h * dh:(h + 1) * dh]
        kn_ref[:, h * dh:(h + 1) * dh] = _rms(kh, gk_ref[...]).astype(kn_ref.dtype)
    mv_ref[...] = kv_ref[:, xw:].astype(mv_ref.dtype)


def _kv_prep(kv, gk, tm):
    t, w2 = kv.shape
    xw = w2 // 2
    return pl.pallas_call(
        _kv_prep_kernel,
        grid=(t // tm,),
        in_specs=[pl.BlockSpec((tm, w2), lambda i: (i, 0)), pl.BlockSpec((1, xw // X_HEADS), lambda i: (0, 0))],
        out_specs=[pl.BlockSpec((tm, xw), lambda i: (i, 0)), pl.BlockSpec((tm, xw), lambda i: (i, 0))],
        out_shape=[jax.ShapeDtypeStruct((t, xw), BF16), jax.ShapeDtypeStruct((t, xw), BF16)],
        compiler_params=_params("parallel"),
        name="kv_prep",
    )(kv, gk.reshape(1, -1))


def _xattn_kernel(q_ref, kn_ref, mv_ref, gq_ref, o_ref):
    xw = q_ref.shape[2]
    dh = xw // X_HEADS
    for h in range(X_HEADS):
        sl = slice(h * dh, (h + 1) * dh)
        qn = _rms(q_ref[0, :, sl], gq_ref[...]).astype(BF16)
        logits = _nt_dot(qn, kn_ref[0, :, sl]) * (dh ** -0.5)
        e = jnp.exp(logits - jnp.max(logits, axis=-1, keepdims=True))
        p = e / jnp.sum(e, axis=-1, keepdims=True)
        o_ref[0, :, sl] = _dot(p.astype(BF16), mv_ref[0, :, sl]).astype(o_ref.dtype)


def _cross_attention(xq, kn, mv, gq, tm):
    b, s, w = xq.shape
    m = kn.shape[1]
    return pl.pallas_call(
        _xattn_kernel,
        grid=(b, s // tm),
        in_specs=[
            pl.BlockSpec((1, tm, w), lambda bi, i: (bi, i, 0)),
            pl.BlockSpec((1, m, w), lambda bi, i: (bi, 0, 0)),
            pl.BlockSpec((1, m, w), lambda bi, i: (bi, 0, 0)),
            pl.BlockSpec((1, w // X_HEADS), lambda bi, i: (0, 0)),
        ],
        out_specs=pl.BlockSpec((1, tm, w), lambda bi, i: (bi, i, 0)),
        out_shape=jax.ShapeDtypeStruct((b, s, w), BF16),
        compiler_params=_params("parallel", "parallel"),
        name="cross_attention",
    )(xq, kn, mv, gq.reshape(1, -1))


def _merge_kernel(x_ref, h_ref, ysb_ref, yml_ref, yx_ref, wg_ref, bg_ref, wsb_ref, wml_ref, wx_ref, wo_ref, o_ref):
    d = x_ref.shape[1]
    h = h_ref[...]
    mixed = None
    for i, (y_ref, w_ref) in enumerate(((ysb_ref, wsb_ref), (yml_ref, wml_ref), (yx_ref, wx_ref))):
        gate = jax.nn.sigmoid(_dot(h, wg_ref[:, i * d:(i + 1) * d]) + bg_ref[:, i * d:(i + 1) * d])
        term = gate * _dot(y_ref[...], w_ref[...])
        mixed = term if mixed is None else mixed + term
    o_ref[...] = x_ref[...] + _dot(mixed.astype(BF16), wo_ref[...])


def _merge(x2d, h, ysb, yml, yx, wg, bg, wsb, wml, wx, wo, tm):
    t, d = x2d.shape
    row = lambda: pl.BlockSpec((tm, d), lambda i: (i, 0))
    const = lambda shape: pl.BlockSpec(shape, lambda i: (0, 0), pipeline_mode=pl.Buffered(1))
    return pl.pallas_call(
        _merge_kernel,
        grid=(t // tm,),
        in_specs=[row(), row(), row(), row(), row(), const((d, N_BRANCH * d)), const((1, N_BRANCH * d)),
                  const((d, d)), const((d, d)), const((d, d)), const((d, d))],
        out_specs=row(),
        out_shape=jax.ShapeDtypeStruct((t, d), F32),
        compiler_params=_params("parallel"),
        name="merge_out",
    )(x2d, h, ysb, yml, yx, wg, bg.reshape(1, -1), wsb, wml, wx, wo)


def _mlp_kernel(x_ref, g_ref, w1_ref, w2_ref, o_ref, *, ff_chunk):
    x = x_ref[...]
    hb = _rms(x, g_ref[...]).astype(BF16)
    acc = x
    for c in range(w1_ref.shape[1] // ff_chunk):
        u = _dot(hb, w1_ref[:, c * ff_chunk:(c + 1) * ff_chunk])
        a = jnp.square(jnp.maximum(u, 0.0)).astype(BF16)
        acc = acc + _dot(a, w2_ref[c * ff_chunk:(c + 1) * ff_chunk, :])
    o_ref[...] = acc


def _mlp(x2d, g, w1, w2, tm, ff_chunk):
    t, d = x2d.shape
    ff = w1.shape[1]
    const = lambda shape: pl.BlockSpec(shape, lambda i: (0, 0), pipeline_mode=pl.Buffered(1))
    return pl.pallas_call(
        functools.partial(_mlp_kernel, ff_chunk=ff_chunk),
        grid=(t // tm,),
        in_specs=[pl.BlockSpec((tm, d), lambda i: (i, 0)), const((1, d)), const((d, ff)), const((ff, d))],
        out_specs=pl.BlockSpec((tm, d), lambda i: (i, 0)),
        out_shape=jax.ShapeDtypeStruct((t, d), F32),
        compiler_params=_params("parallel"),
        name="mlp",
    )(x2d, g.reshape(1, d), w1, w2)


def _layer(x, mem, g_mix, w_in, b_if, b_gate, conv_w, conv_b, ml_norm_g, g_mem, w_mem_kv, q_norm_g, k_norm_g,
           w_sb_proj, w_ml_proj, w_x_proj, w_out, g_mlp, w_ff1, w_ff2):
    b, s, d = x.shape
    m = mem.shape[1]
    t = b * s
    sb_w = w_sb_proj.shape[0]
    ml_w = w_ml_proj.shape[0]
    x_w = w_x_proj.shape[0]
    sb_heads = sb_w // SB_HD
    tm = min(512, t)

    sizes = (sb_w, sb_w, sb_w, ml_w, ml_w, ml_w, ml_w, ML_HEADS, ML_HEADS, x_w, N_BRANCH * d)
    offs = [0]
    for sz in sizes:
        offs.append(offs[-1] + sz)
    col = lambda a, bnd: w_in[:, offs[a]:offs[bnd]]

    x2d = x.reshape(t, d)
    h = _norm_cast(x2d, g_mix, tm)

    qkv = _matmul(h, col(0, 3).astype(BF16), BF16, tm, min(1024, 3 * sb_w), "proj_sb")
    mlqk = _matmul(h, col(3, 5).astype(BF16), F32, tm, min(1024, 2 * ml_w), "proj_ml_qk")
    mlv = _matmul(h, col(5, 6).astype(BF16), BF16, tm, min(1024, ml_w), "proj_ml_v")
    mlo = _matmul(h, col(6, 7).astype(BF16), F32, tm, min(1024, ml_w), "proj_ml_o")
    w_if = jnp.pad(col(7, 9), ((0, 0), (0, 128 - 2 * ML_HEADS))).astype(BF16)
    gates_if = _matmul(h, w_if, F32, tm, 128, "proj_ml_if")[:, :2 * ML_HEADS] + b_if
    xq = _matmul(h, col(9, 10).astype(BF16), F32, tm, min(1024, x_w), "proj_xq")

    y_sb = _sb_attention(qkv.reshape(b, s, 3 * sb_w), sb_heads).reshape(t, sb_w)

    gif = gates_if.reshape(b, s, 2, ML_HEADS)
    gcol = gif.transpose(0, 3, 1, 2)
    grow = gif.transpose(0, 3, 2, 1)
    y_ml = _mlstm(mlqk.reshape(b, s, 2 * ml_w), mlv.reshape(b, s, ml_w), mlo.reshape(b, s, ml_w),
                  gcol, grow, conv_w, conv_b, ml_norm_g).reshape(t, ml_w)

    tmm = min(512, b * m)
    hm = _norm_cast(mem.reshape(b * m, d), g_mem, tmm)
    kv = _matmul(hm, w_mem_kv.astype(BF16), F32, tmm, min(1024, 2 * x_w), "proj_mem_kv")
    kn, mv = _kv_prep(kv, k_norm_g, tmm)
    y_x = _cross_attention(xq.reshape(b, s, x_w), kn.reshape(b, m, x_w), mv.reshape(b, m, x_w), q_norm_g,
                           min(512, s)).reshape(t, x_w)

    x1 = _merge(x2d, h, y_sb, y_ml, y_x, col(10, 11).astype(BF16), b_gate, w_sb_proj.astype(BF16),
                w_ml_proj.astype(BF16), w_x_proj.astype(BF16), w_out.astype(BF16), min(256, t))
    x2 = _mlp(x1, g_mlp, w_ff1.astype(BF16), w_ff2.astype(BF16), tm, min(1024, w_ff1.shape[1]))
    return x2.reshape(b, s, d)


def kernel(x, mem, g_mix, w_in, b_if, b_gate, conv_w, conv_b, ml_norm_g, g_mem, w_mem_kv, q_norm_g, k_norm_g,
           w_sb_proj, w_ml_proj, w_x_proj, w_out, g_mlp, w_ff1, w_ff2):
    for l in range(g_mix.shape[0]):
        x = _layer(x, mem, g_mix[l], w_in[l], b_if[l], b_gate[l], conv_w[l], conv_b[l], ml_norm_g[l], g_mem[l],
                   w_mem_kv[l], q_norm_g[l], k_norm_g[l], w_sb_proj[l], w_ml_proj[l], w_x_proj[l], w_out[l],
                   g_mlp[l], w_ff1[l], w_ff2[l])
    return x
```

```python
import functools

import jax
import jax.numpy as jnp
from jax import lax
from jax.experimental import pallas as pl
from jax.experimental.pallas import tpu as pltpu

EPS = 1e-6
LOG2E = 1.4426950408889634
SB_HD = 128
ML_HEADS = 4
X_HEADS = 4
CONV_W = 4
N_BRANCH = 3
SB_BLOCK = 128
SB_QBLOCKS = 8
SB_KBLOCKS = 4
ML_CHUNK = 256
CONV_HALO = 8
VMEM_LIMIT = 56 * 1024 * 1024

F32 = jnp.float32
BF16 = jnp.bfloat16


def _params(*sem):
    return pltpu.CompilerParams(dimension_semantics=sem, vmem_limit_bytes=VMEM_LIMIT)


def _nt_dot(a, b):
    return lax.dot_general(a, b, (((1,), (1,)), ((), ())), preferred_element_type=F32)


def _tn_dot(a, b):
    return lax.dot_general(a, b, (((0,), (0,)), ((), ())), preferred_element_type=F32)


def _dot(a, b):
    return jnp.dot(a, b, preferred_element_type=F32)


def _rms(x, g):
    return x * lax.rsqrt(jnp.mean(x * x, axis=-1, keepdims=True) + EPS) * g


def _norm_kernel(x_ref, g_ref, o_ref):
    o_ref[...] = _rms(x_ref[...], g_ref[...]).astype(o_ref.dtype)


def _norm_cast(x2d, g, tm):
    t, d = x2d.shape
    return pl.pallas_call(
        _norm_kernel,
        grid=(t // tm,),
        in_specs=[pl.BlockSpec((tm, d), lambda i: (i, 0)), pl.BlockSpec((1, d), lambda i: (0, 0))],
        out_specs=pl.BlockSpec((tm, d), lambda i: (i, 0)),
        out_shape=jax.ShapeDtypeStruct((t, d), BF16),
        compiler_params=_params("parallel"),
        name="norm_cast",
    )(x2d, g.reshape(1, d))


def _matmul_kernel(a_ref, w_ref, o_ref):
    o_ref[...] = _dot(a_ref[...], w_ref[...]).astype(o_ref.dtype)


def _matmul(a, w, out_dtype, tm, tn, name):
    t, k = a.shape
    n = w.shape[1]
    return pl.pallas_call(
        _matmul_kernel,
        grid=(n // tn, t // tm),
        in_specs=[pl.BlockSpec((tm, k), lambda j, i: (i, 0)), pl.BlockSpec((k, tn), lambda j, i: (0, j))],
        out_specs=pl.BlockSpec((tm, tn), lambda j, i: (i, j)),
        out_shape=jax.ShapeDtypeStruct((t, n), out_dtype),
        compiler_params=_params("parallel", "parallel"),
        name=name,
    )(a, w)


def _sb_kernel(q_ref, k_ref, v_ref, tri_ref, o_ref, acc_ref, carry_ref, *, scale):
    blk = SB_BLOCK
    tq = blk * SB_QBLOCKS
    tk = blk * SB_KBLOCKS
    qi = pl.program_id(2)

    def group(start, row_off):
        diag = row_off is not None
        kg = k_ref[0, pl.ds(start, tk), :]
        vg = v_ref[0, pl.ds(start, tk), :]
        z_all = _nt_dot(q_ref[0], kg)
        a_blocks = [None] * SB_KBLOCKS
        for j in reversed(range(SB_KBLOCKS)):
            r0 = (row_off + j) * blk if diag else 0
            z = z_all[r0:, j * blk:(j + 1) * blk] * (scale * LOG2E)
            t = jnp.maximum(z, 0.0)
            mn = jnp.minimum(z, 0.0)
            l = jnp.log(1.0 + jnp.exp2(mn - t)) * LOG2E
            sp = t + l
            log_beta = mn - l
            if diag:
                causal = (lax.broadcasted_iota(jnp.int32, z.shape, 1) < lax.broadcasted_iota(jnp.int32, z.shape, 0))
                sp = jnp.where(causal, sp, 0.0)
            hi = sp.astype(BF16)
            lo = (sp - hi.astype(F32)).astype(BF16)
            r = _dot(jnp.concatenate([hi, lo], axis=1), tri_ref[...])
            carry = carry_ref[r0:, :]
            a = jnp.exp2(log_beta - r[:, :blk] - carry)
            if diag:
                a = jnp.where(causal, a, 0.0)
            carry_ref[r0:, :] = carry + r[:, blk:]
            a = a.astype(BF16)
            a_blocks[j] = a if r0 == 0 else jnp.concatenate([jnp.zeros((r0, blk), BF16), a], axis=0)
        return _dot(jnp.concatenate(a_blocks, axis=1), vg)

    carry_ref[...] = jnp.zeros_like(carry_ref)
    acc_ref[...] = jnp.zeros_like(acc_ref)
    n_diag = SB_QBLOCKS // SB_KBLOCKS
    for d in reversed(range(n_diag)):
        acc_ref[...] += group(pl.multiple_of(qi * tq + d * tk, tk), d * SB_KBLOCKS)

    def body(i, c):
        acc_ref[...] += group(pl.multiple_of((qi * n_diag - 1 - i) * tk, tk), None)
        return c

    lax.fori_loop(0, qi * n_diag, body, 0)
    o_ref[0] = acc_ref[...].astype(o_ref.dtype)


def _sb_attention(qkv, n_heads):
    b, s, _ = qkv.shape
    blk = SB_BLOCK
    tq = blk * SB_QBLOCKS
    row = lax.broadcasted_iota(jnp.int32, (2 * blk, 2 * blk), 0) % blk
    col = lax.broadcasted_iota(jnp.int32, (2 * blk, 2 * blk), 1)
    tri = jnp.where((col >= blk) | (row > col), 1.0, 0.0).astype(BF16)
    return pl.pallas_call(
        functools.partial(_sb_kernel, scale=SB_HD ** -0.5),
        grid=(b, n_heads, s // tq),
        in_specs=[
            pl.BlockSpec((1, tq, SB_HD), lambda bi, h, i: (bi, i, h)),
            pl.BlockSpec((1, s, SB_HD), lambda bi, h, i: (bi, 0, n_heads + h)),
            pl.BlockSpec((1, s, SB_HD), lambda bi, h, i: (bi, 0, 2 * n_heads + h)),
            pl.BlockSpec((2 * blk, 2 * blk), lambda bi, h, i: (0, 0)),
        ],
        out_specs=pl.BlockSpec((1, tq, SB_HD), lambda bi, h, i: (bi, i, h)),
        out_shape=jax.ShapeDtypeStruct((b, s, n_heads * SB_HD), BF16),
        scratch_shapes=[pltpu.VMEM((tq, SB_HD), F32), pltpu.VMEM((tq, blk), F32)],
        compiler_params=_params("parallel", "parallel", "arbitrary"),
        name="sb_attention",
    )(qkv, qkv, qkv, tri)


def _mlstm_kernel(q_ref, k_ref, v_ref, o_ref, gcol_ref, grow_ref, wq_ref, wk_ref, bq_ref, bk_ref, ng_ref,
                  y_ref, qext, kext, ct_ref, n_ref, m_ref, *, k_scale):
    L = ML_CHUNK
    H = CONV_HALO
    c = pl.program_id(2)

    @pl.when(c == 0)
    def _():
        qext[0:H, :] = jnp.zeros((H, qext.shape[1]), F32)
        kext[0:H, :] = jnp.zeros((H, kext.shape[1]), F32)
        ct_ref[...] = jnp.zeros_like(ct_ref)
        n_ref[...] = jnp.zeros_like(n_ref)
        m_ref[...] = jnp.zeros_like(m_ref)

    def conv_silu(raw_ref, ext, w_ref, b_ref):
        raw = raw_ref[0]
        ext[H:H + L, :] = raw
        u = b_ref[...] + w_ref[CONV_W - 1:CONV_W, :] * raw
        for j in range(CONV_W - 1):
            shift = CONV_W - 1 - j
            u = u + w_ref[j:j + 1, :] * ext[H - shift:H - shift + L, :]
        ext[0:H, :] = raw[L - H:L, :]
        return u * jax.nn.sigmoid(u)

    q = conv_silu(q_ref, qext, wq_ref, bq_ref)
    k = conv_silu(k_ref, kext, wk_ref, bk_ref) * k_scale
    v = v_ref[0]
    qb = q.astype(BF16)
    kb = k.astype(BF16)

    i_col = gcol_ref[0, 0, :, 0:1]
    lf_col = jax.nn.log_sigmoid(gcol_ref[0, 0, :, 1:2])
    i_row = grow_ref[0, 0, 0:1, :]
    lf_row = jax.nn.log_sigmoid(grow_ref[0, 0, 1:2, :])

    t_idx = lax.broadcasted_iota(jnp.int32, (L, L), 0)
    s_idx = lax.broadcasted_iota(jnp.int32, (L, L), 1)
    tril = s_idx <= t_idx
    bcum_col = jnp.sum(jnp.where(tril, lf_row, 0.0), axis=1, keepdims=True)
    bcum_row = jnp.sum(jnp.where(t_idx <= s_idx, lf_col, 0.0), axis=0, keepdims=True)
    m_prev = m_ref[...]
    d = jnp.where(tril, bcum_col - bcum_row + i_row, -jnp.inf)
    m_inter = bcum_col + m_prev
    m_t = jnp.maximum(m_inter, jnp.max(d, axis=1, keepdims=True))
    w = jnp.exp(d - m_t)
    s_inter = jnp.exp(m_inter - m_t)

    ct = ct_ref[...]
    n_prev = n_ref[...]
    sc = _nt_dot(qb, kb) * w
    num = _dot(sc.astype(BF16), v) + s_inter * _dot(qb, ct.astype(BF16))
    den = jnp.sum(sc, axis=1, keepdims=True) + s_inter * jnp.sum(q * n_prev, axis=1, keepdims=True)
    h = num / jnp.maximum(jnp.abs(den), jnp.exp(-m_t))

    b_end = jnp.sum(lf_row, axis=1, keepdims=True)
    g_col = b_end - bcum_col + i_col
    m_new = jnp.maximum(b_end + m_prev, jnp.max(g_col, axis=0, keepdims=True))
    decay = jnp.exp(b_end + m_prev - m_new)
    wk = jnp.exp(g_col - m_new)
    vw = (v.astype(F32) * wk).astype(BF16)
    ct_ref[...] = decay * ct + _tn_dot(kb, vw)
    n_ref[...] = decay * n_prev + jnp.sum(k * wk, axis=0, keepdims=True)
    m_ref[...] = m_new

    hn = _rms(h, ng_ref[...])
    y_ref[0] = (hn * jax.nn.sigmoid(o_ref[0])).astype(y_ref.dtype)


def _mlstm(mlqk, mlv, mlo, gcol, grow, conv_w, conv_b, norm_g):
    b, s, w2 = mlqk.shape
    wd = w2 // 2
    nh = ML_HEADS
    dh = wd // nh
    L = ML_CHUNK
    blk = lambda off: pl.BlockSpec((1, L, dh), lambda bi, h, c: (bi, c, off + h))
    return pl.pallas_call(
        functools.partial(_mlstm_kernel, k_scale=dh ** -0.5),
        grid=(b, nh, s // L),
        in_specs=[
            blk(0), blk(nh), blk(0), blk(0),
            pl.BlockSpec((1, 1, L, 2), lambda bi, h, c: (bi, h, c, 0)),
            pl.BlockSpec((1, 1, 2, L), lambda bi, h, c: (bi, h, 0, c)),
            pl.BlockSpec((CONV_W, dh), lambda bi, h, c: (0, h)),
            pl.BlockSpec((CONV_W, dh), lambda bi, h, c: (0, nh + h)),
            pl.BlockSpec((1, dh), lambda bi, h, c: (0, h)),
            pl.BlockSpec((1, dh), lambda bi, h, c: (0, nh + h)),
            pl.BlockSpec((1, dh), lambda bi, h, c: (0, h)),
        ],
        out_specs=blk(0),
        out_shape=jax.ShapeDtypeStruct((b, s, wd), BF16),
        scratch_shapes=[
            pltpu.VMEM((L + CONV_HALO, dh), F32),
            pltpu.VMEM((L + CONV_HALO, dh), F32),
            pltpu.VMEM((dh, dh), F32),
            pltpu.VMEM((1, dh), F32),
            pltpu.VMEM((1, 1), F32),
        ],
        compiler_params=_params("parallel", "parallel", "arbitrary"),
        name="mlstm",
    )(mlqk, mlqk, mlv, mlo, gcol, grow, conv_w, conv_w, conv_b.reshape(1, w2), conv_b.reshape(1, w2),
      norm_g.reshape(1, wd))


def _kv_prep_kernel(kv_ref, gk_ref, kn_ref, mv_ref):
    xw = kn_ref.shape[1]
    dh = xw // X_HEADS
    for h in range(X_HEADS):
        kh = kv_ref[:, h * dh:(h + 1) * dh]
        kn_ref[:, h * dh:(h + 1) * dh] = _rms(kh, gk_ref[...]).astype(kn_ref.dtype)
    mv_ref[...] = kv_ref[:, xw:].astype(mv_ref.dtype)


def _kv_prep(kv, gk, tm):
    t, w2 = kv.shape
    xw = w2 // 2
    return pl.pallas_call(
        _kv_prep_kernel,
        grid=(t // tm,),
        in_specs=[pl.BlockSpec((tm, w2), lambda i: (i, 0)), pl.BlockSpec((1, xw // X_HEADS), lambda i: (0, 0))],
        out_specs=[pl.BlockSpec((tm, xw), lambda i: (i, 0)), pl.BlockSpec((tm, xw), lambda i: (i, 0))],
        out_shape=[jax.ShapeDtypeStruct((t, xw), BF16), jax.ShapeDtypeStruct((t, xw), BF16)],
        compiler_params=_params("parallel"),
        name="kv_prep",
    )(kv, gk.reshape(1, -1))


def _xattn_kernel(q_ref, kn_ref, mv_ref, gq_ref, o_ref):
    xw = q_ref.shape[2]
    dh = xw // X_HEADS
    for h in range(X_HEADS):
        sl = slice(h * dh, (h + 1) * dh)
        qn = _rms(q_ref[0, :, sl], gq_ref[...]).astype(BF16)
        logits = _nt_dot(qn, kn_ref[0, :, sl]) * (dh ** -0.5)
        e = jnp.exp(logits - jnp.max(logits, axis=-1, keepdims=True))
        p = e / jnp.sum(e, axis=-1, keepdims=True)
        o_ref[0, :, sl] = _dot(p.astype(BF16), mv_ref[0, :, sl]).astype(o_ref.dtype)


def _cross_attention(xq, kn, mv, gq, tm):
    b, s, w = xq.shape
    m = kn.shape[1]
    return pl.pallas_call(
        _xattn_kernel,
        grid=(b, s // tm),
        in_specs=[
            pl.BlockSpec((1, tm, w), lambda bi, i: (bi, i, 0)),
            pl.BlockSpec((1, m, w), lambda bi, i: (bi, 0, 0)),
            pl.BlockSpec((1, m, w), lambda bi, i: (bi, 0, 0)),
            pl.BlockSpec((1, w // X_HEADS), lambda bi, i: (0, 0)),
        ],
        out_specs=pl.BlockSpec((1, tm, w), lambda bi, i: (bi, i, 0)),
        out_shape=jax.ShapeDtypeStruct((b, s, w), BF16),
        compiler_params=_params("parallel", "parallel"),
        name="cross_attention",
    )(xq, kn, mv, gq.reshape(1, -1))


def _merge_kernel(x_ref, h_ref, ysb_ref, yml_ref, yx_ref, wg_ref, bg_ref, wsb_ref, wml_ref, wx_ref, wo_ref, o_ref):
    d = x_ref.shape[1]
    h = h_ref[...]
    mixed = None
    for i, (y_ref, w_ref) in enumerate(((ysb_ref, wsb_ref), (yml_ref, wml_ref), (yx_ref, wx_ref))):
        gate = jax.nn.sigmoid(_dot(h, wg_ref[:, i * d:(i + 1) * d]) + bg_ref[:, i * d:(i + 1) * d])
        term = gate * _dot(y_ref[...], w_ref[...])
        mixed = term if mixed is None else mixed + term
    o_ref[...] = x_ref[...] + _dot(mixed.astype(BF16), wo_ref[...])


def _merge(x2d, h, ysb, yml, yx, wg, bg, wsb, wml, wx, wo, tm):
    t, d = x2d.shape
    row = lambda: pl.BlockSpec((tm, d), lambda i: (i, 0))
    const = lambda shape: pl.BlockSpec(shape, lambda i: (0, 0), pipeline_mode=pl.Buffered(1))
    return pl.pallas_call(
        _merge_kernel,
        grid=(t // tm,),
        in_specs=[row(), row(), row(), row(), row(), const((d, N_BRANCH * d)), const((1, N_BRANCH * d)),
                  const((d, d)), const((d, d)), const((d, d)), const((d, d))],
        out_specs=row(),
        out_shape=jax.ShapeDtypeStruct((t, d), F32),
        compiler_params=_params("parallel"),
        name="merge_out",
    )(x2d, h, ysb, yml, yx, wg, bg.reshape(1, -1), wsb, wml, wx, wo)


def _mlp_kernel(x_ref, g_ref, w1_ref, w2_ref, o_ref, *, ff_chunk):
    x = x_ref[...]
    hb = _rms(x, g_ref[...]).astype(BF16)
    acc = x
    for c in range(w1_ref.shape[1] // ff_chunk):
        u = _dot(hb, w1_ref[:, c * ff_chunk:(c + 1) * ff_chunk])
        a = jnp.square(jnp.maximum(u, 0.0)).astype(BF16)
        acc = acc + _dot(a, w2_ref[c * ff_chunk:(c + 1) * ff_chunk, :])
    o_ref[...] = acc


def _mlp(x2d, g, w1, w2, tm, ff_chunk):
    t, d = x2d.shape
    ff = w1.shape[1]
    const = lambda shape: pl.BlockSpec(shape, lambda i: (0, 0), pipeline_mode=pl.Buffered(1))
    return pl.pallas_call(
        functools.partial(_mlp_kernel, ff_chunk=ff_chunk),
        grid=(t // tm,),
        in_specs=[pl.BlockSpec((tm, d), lambda i: (i, 0)), const((1, d)), const((d, ff)), const((ff, d))],
        out_specs=pl.BlockSpec((tm, d), lambda i: (i, 0)),
        out_shape=jax.ShapeDtypeStruct((t, d), F32),
        compiler_params=_params("parallel"),
        name="mlp",
    )(x2d, g.reshape(1, d), w1, w2)


def _layer(x, mem, g_mix, w_in, b_if, b_gate, conv_w, conv_b, ml_norm_g, g_mem, w_mem_kv, q_norm_g, k_norm_g,
           w_sb_proj, w_ml_proj, w_x_proj, w_out, g_mlp, w_ff1, w_ff2):
    b, s, d = x.shape
    m = mem.shape[1]
    t = b * s
    sb_w = w_sb_proj.shape[0]
    ml_w = w_ml_proj.shape[0]
    x_w = w_x_proj.shape[0]
    sb_heads = sb_w // SB_HD
    tm = min(512, t)

    sizes = (sb_w, sb_w, sb_w, ml_w, ml_w, ml_w, ml_w, ML_HEADS, ML_HEADS, x_w, N_BRANCH * d)
    offs = [0]
    for sz in sizes:
        offs.append(offs[-1] + sz)
    col = lambda a, bnd: w_in[:, offs[a]:offs[bnd]]

    x2d = x.reshape(t, d)
    h = _norm_cast(x2d, g_mix, tm)

    qkv = _matmul(h, col(0, 3).astype(BF16), BF16, tm, min(1024, 3 * sb_w), "proj_sb")
    mlqk = _matmul(h, col(3, 5).astype(BF16), F32, tm, min(1024, 2 * ml_w), "proj_ml_qk")
    mlv = _matmul(h, col(5, 6).astype(BF16), BF16, tm, min(1024, ml_w), "proj_ml_v")
    mlo = _matmul(h, col(6, 7).astype(BF16), F32, tm, min(1024, ml_w), "proj_ml_o")
    w_if = jnp.pad(col(7, 9), ((0, 0), (0, 128 - 2 * ML_HEADS))).astype(BF16)
    gates_if = _matmul(h, w_if, F32, tm, 128, "proj_ml_if")[:, :2 * ML_HEADS] + b_if
    xq = _matmul(h, col(9, 10).astype(BF16), F32, tm, min(1024, x_w), "proj_xq")

    y_sb = _sb_attention(qkv.reshape(b, s, 3 * sb_w), sb_heads).reshape(t, sb_w)

    gif = gates_if.reshape(b, s, 2, ML_HEADS)
    gcol = gif.transpose(0, 3, 1, 2)
    grow = gif.transpose(0, 3, 2, 1)
    y_ml = _mlstm(mlqk.reshape(b, s, 2 * ml_w), mlv.reshape(b, s, ml_w), mlo.reshape(b, s, ml_w),
                  gcol, grow, conv_w, conv_b, ml_norm_g).reshape(t, ml_w)

    tmm = min(512, b * m)
    hm = _norm_cast(mem.reshape(b * m, d), g_mem, tmm)
    kv = _matmul(hm, w_mem_kv.astype(BF16), F32, tmm, min(1024, 2 * x_w), "proj_mem_kv")
    kn, mv = _kv_prep(kv, k_norm_g, tmm)
    y_x = _cross_attention(xq.reshape(b, s, x_w), kn.reshape(b, m, x_w), mv.reshape(b, m, x_w), q_norm_g,
                           min(512, s)).reshape(t, x_w)

    x1 = _merge(x2d, h, y_sb, y_ml, y_x, col(10, 11).astype(BF16), b_gate, w_sb_proj.astype(BF16),
                w_ml_proj.astype(BF16), w_x_proj.astype(BF16), w_out.astype(BF16), min(256, t))
    x2 = _mlp(x1, g_mlp, w_ff1.astype(BF16), w_ff2.astype(BF16), tm, min(1024, w_ff1.shape[1]))
    return x2.reshape(b, s, d)


def kernel(x, mem, g_mix, w_in, b_if, b_gate, conv_w, conv_b, ml_norm_g, g_mem, w_mem_kv, q_norm_g, k_norm_g,
           w_sb_proj, w_ml_proj, w_x_proj, w_out, g_mlp, w_ff1, w_ff2):
    for l in range(g_mix.shape[0]):
        x = _layer(x, mem, g_mix[l], w_in[l], b_if[l], b_gate[l], conv_w[l], conv_b[l], ml_norm_g[l], g_mem[l],
                   w_mem_kv[l], q_norm_g[l], k_norm_g[l], w_sb_proj[l], w_ml_proj[l], w_x_proj[l], w_out[l],
                   g_mlp[l], w_ff1[l], w_ff2[l])
    return x
```

```python
import functools

import jax
import jax.numpy as jnp
from jax import lax
from jax.experimental import pallas as pl
from jax.experimental.pallas import tpu as pltpu

EPS = 1e-6
LOG2E = 1.4426950408889634
SB_HD = 128
ML_HEADS = 4
X_HEADS = 4
CONV_W = 4
N_BRANCH = 3
SB_BLOCK = 128
SB_QBLOCKS = 8
SB_KBLOCKS = 4
INPROJ_COLS = 1024
ML_CHUNK = 256
CONV_HALO = 8
VMEM_LIMIT = 56 * 1024 * 1024

F32 = jnp.float32
BF16 = jnp.bfloat16


def _params(*sem):
    return pltpu.CompilerParams(dimension_semantics=sem, vmem_limit_bytes=VMEM_LIMIT)


def _nt_dot(a, b):
    return lax.dot_general(a, b, (((1,), (1,)), ((), ())), preferred_element_type=F32)


def _tn_dot(a, b):
    return lax.dot_general(a, b, (((0,), (0,)), ((), ())), preferred_element_type=F32)


def _dot(a, b):
    return jnp.dot(a, b, preferred_element_type=F32)


def _rms(x, g):
    return x * lax.rsqrt(jnp.mean(x * x, axis=-1, keepdims=True) + EPS) * g


def _norm_kernel(x_ref, g_ref, o_ref):
    o_ref[...] = _rms(x_ref[...], g_ref[...]).astype(o_ref.dtype)


def _norm_cast(x2d, g, tm):
    t, d = x2d.shape
    return pl.pallas_call(
        _norm_kernel,
        grid=(t // tm,),
        in_specs=[pl.BlockSpec((tm, d), lambda i: (i, 0)), pl.BlockSpec((1, d), lambda i: (0, 0))],
        out_specs=pl.BlockSpec((tm, d), lambda i: (i, 0)),
        out_shape=jax.ShapeDtypeStruct((t, d), BF16),
        compiler_params=_params("parallel"),
        name="norm_cast",
    )(x2d, g.reshape(1, d))


def _matmul_kernel(a_ref, w_ref, o_ref):
    o_ref[...] = _dot(a_ref[...], w_ref[...]).astype(o_ref.dtype)


def _matmul(a, w, out_dtype, tm, tn, name):
    t, k = a.shape
    n = w.shape[1]
    return pl.pallas_call(
        _matmul_kernel,
        grid=(n // tn, t // tm),
        in_specs=[pl.BlockSpec((tm, k), lambda j, i: (i, 0)), pl.BlockSpec((k, tn), lambda j, i: (0, j))],
        out_specs=pl.BlockSpec((tm, tn), lambda j, i: (i, j)),
        out_shape=jax.ShapeDtypeStruct((t, n), out_dtype),
        compiler_params=_params("parallel", "parallel"),
        name=name,
    )(a, w)


def _inproj_kernel(x_ref, g_ref, w_ref, gq_ref, qkv_ref, mlqk_ref, mlv_ref, mlo_ref, gif_ref, qn_ref, *, widths):
    hb = _rms(x_ref[...], g_ref[...]).astype(BF16)
    sb3, ml2, mlv_w, mlo_w, xq_w, _ = widths
    step = INPROJ_COLS

    def proj(c0, c1):
        return _dot(hb, w_ref[:, c0:c1])

    def emit(out_ref, base, width):
        for c in range(0, width, step):
            out_ref[:, c:c + step] = proj(base + c, base + c + step).astype(out_ref.dtype)

    emit(qkv_ref, 0, sb3)
    emit(mlqk_ref, sb3, ml2)
    emit(mlv_ref, sb3 + ml2, mlv_w)
    emit(mlo_ref, sb3 + ml2 + mlv_w, mlo_w)
    base = sb3 + ml2 + mlv_w + mlo_w
    dh = xq_w // X_HEADS
    for h in range(X_HEADS):
        qn_ref[:, h * dh:(h + 1) * dh] = _rms(proj(base + h * dh, base + (h + 1) * dh), gq_ref[...]).astype(BF16)
    gif_ref[...] = proj(base + xq_w, base + xq_w + gif_ref.shape[1])


def _inproj(x2d, g, w_cat, gq, widths, tm):
    t, d = x2d.shape
    sb3, ml2, mlv_w, mlo_w, xq_w, gif_w = widths
    row = lambda n: pl.BlockSpec((tm, n), lambda i: (i, 0))
    const = lambda shape: pl.BlockSpec(shape, lambda i: (0, 0), pipeline_mode=pl.Buffered(1))
    return pl.pallas_call(
        functools.partial(_inproj_kernel, widths=widths),
        grid=(t // tm,),
        in_specs=[row(d), const((1, d)), const(w_cat.shape), const((1, xq_w // X_HEADS))],
        out_specs=[row(sb3), row(ml2), row(mlv_w), row(mlo_w), row(gif_w), row(xq_w)],
        out_shape=[jax.ShapeDtypeStruct((t, sb3), BF16), jax.ShapeDtypeStruct((t, ml2), F32),
                   jax.ShapeDtypeStruct((t, mlv_w), BF16), jax.ShapeDtypeStruct((t, mlo_w), F32),
                   jax.ShapeDtypeStruct((t, gif_w), F32), jax.ShapeDtypeStruct((t, xq_w), BF16)],
        compiler_params=_params("parallel"),
        name="in_proj",
    )(x2d, g.reshape(1, d), w_cat, gq.reshape(1, -1))


def _sb_kernel(q_ref, k_ref, v_ref, tri_ref, o_ref, acc_ref, carry_ref, z_ref, a_ref, *, scale):
    blk = SB_BLOCK
    tq = blk * SB_QBLOCKS
    tk = blk * SB_KBLOCKS
    qi = pl.program_id(2)

    def key_start(g):
        return pl.multiple_of(g * tk, tk)

    def mask_top(x):
        causal = (lax.broadcasted_iota(jnp.int32, (blk, blk), 1) < lax.broadcasted_iota(jnp.int32, (blk, blk), 0))
        top = jnp.where(causal, x[:blk], 0.0)
        return top if x.shape[0] == blk else jnp.concatenate([top, x[blk:]], axis=0)

    def logits(g, r_base=0):
        return _nt_dot(q_ref[0, r_base:, :], k_ref[0, pl.ds(key_start(g), tk), :])

    def weights(z_all, r_base, diag):
        a_blocks = [None] * SB_KBLOCKS
        for j in reversed(range(SB_KBLOCKS)):
            r0 = j * blk if diag else 0
            z = z_all[r0:, j * blk:(j + 1) * blk] * (scale * LOG2E)
            l = jnp.log(1.0 + jnp.exp2(-jnp.abs(z))) * LOG2E
            sp = jnp.maximum(z, 0.0) + l
            if diag:
                sp = mask_top(sp)
            hi = sp.astype(BF16)
            lo = (sp - hi.astype(F32)).astype(BF16)
            r = _dot(jnp.concatenate([hi, lo], axis=1), tri_ref[...])
            carry = carry_ref[r_base + r0:, :]
            a = jnp.exp2(z - r[:, :blk] - carry)
            if diag:
                a = mask_top(a)
            carry_ref[r_base + r0:, :] = carry + r[:, blk:]
            a = a.astype(BF16)
            a_blocks[j] = a if r0 == 0 else jnp.concatenate([jnp.zeros((r0, blk), BF16), a], axis=0)
        return jnp.concatenate(a_blocks, axis=1)

    def accumulate(a, g, r_base=0):
        acc_ref[r_base:, :] += _dot(a, v_ref[0, pl.ds(key_start(g), tk), :])

    carry_ref[...] = jnp.zeros_like(carry_ref)
    acc_ref[...] = jnp.zeros_like(acc_ref)
    n_diag = SB_QBLOCKS // SB_KBLOCKS
    assert n_diag * SB_KBLOCKS == SB_QBLOCKS and n_diag % 2 == 0
    n_full = qi * n_diag
    z_next = logits(n_full + n_diag - 1, (n_diag - 1) * tk)
    for d in reversed(range(n_diag)):
        z_cur = z_next
        if d > 0:
            z_next = logits(n_full + d - 1, (d - 1) * tk)
        else:
            z_ref[...] = logits(jnp.maximum(n_full - 1, 0))
        a = weights(z_cur, d * tk, True)
        if d > 0:
            accumulate(a, n_full + d, d * tk)
        else:
            a_ref[...] = a

    def body(i, c):
        ga = n_full - 1 - 2 * i
        gb = ga - 1
        z_b = logits(gb)
        accumulate(a_ref[...], ga + 1)
        a_a = weights(z_ref[...], 0, False)
        z_ref[...] = logits(jnp.maximum(ga - 2, 0))
        accumulate(a_a, ga)
        a_ref[...] = weights(z_b, 0, False)
        return c

    lax.fori_loop(0, n_full // 2, body, 0)
    accumulate(a_ref[...], 0)
    o_ref[0] = acc_ref[...].astype(o_ref.dtype)


def _sb_attention(qkv, n_heads):
    b, s, _ = qkv.shape
    blk = SB_BLOCK
    tq = blk * SB_QBLOCKS
    assert s % tq == 0
    row = lax.broadcasted_iota(jnp.int32, (2 * blk, 2 * blk), 0) % blk
    col = lax.broadcasted_iota(jnp.int32, (2 * blk, 2 * blk), 1)
    tri = jnp.where((col >= blk) | (row >= col), 1.0, 0.0).astype(BF16)
    return pl.pallas_call(
        functools.partial(_sb_kernel, scale=SB_HD ** -0.5),
        grid=(b, n_heads, s // tq),
        in_specs=[
            pl.BlockSpec((1, tq, SB_HD), lambda bi, h, i: (bi, i, h)),
            pl.BlockSpec((1, s, SB_HD), lambda bi, h, i: (bi, 0, n_heads + h)),
            pl.BlockSpec((1, s, SB_HD), lambda bi, h, i: (bi, 0, 2 * n_heads + h)),
            pl.BlockSpec((2 * blk, 2 * blk), lambda bi, h, i: (0, 0)),
        ],
        out_specs=pl.BlockSpec((1, tq, SB_HD), lambda bi, h, i: (bi, i, h)),
        out_shape=jax.ShapeDtypeStruct((b, s, n_heads * SB_HD), BF16),
        scratch_shapes=[pltpu.VMEM((tq, SB_HD), F32), pltpu.VMEM((tq, blk), F32),
                        pltpu.VMEM((tq, blk * SB_KBLOCKS), F32), pltpu.VMEM((tq, blk * SB_KBLOCKS), BF16)],
        compiler_params=_params("parallel", "parallel", "arbitrary"),
        name="sb_attention",
    )(qkv, qkv, qkv, tri)


def _mlstm_kernel(q_ref, k_ref, v_ref, o_ref, gcol_ref, grow_ref, wq_ref, wk_ref, bq_ref, bk_ref, ng_ref,
                  y_ref, qext, kext, ct_ref, n_ref, m_ref, *, k_scale):
    L = ML_CHUNK
    H = CONV_HALO
    c = pl.program_id(2)

    @pl.when(c == 0)
    def _():
        qext[0:H, :] = jnp.zeros((H, qext.shape[1]), F32)
        kext[0:H, :] = jnp.zeros((H, kext.shape[1]), F32)
        ct_ref[...] = jnp.zeros_like(ct_ref)
        n_ref[...] = jnp.zeros_like(n_ref)
        m_ref[...] = jnp.zeros_like(m_ref)

    def conv_silu(raw_ref, ext, w_ref, b_ref):
        raw = raw_ref[0]
        ext[H:H + L, :] = raw
        u = b_ref[...] + w_ref[CONV_W - 1:CONV_W, :] * raw
        for j in range(CONV_W - 1):
            shift = CONV_W - 1 - j
            u = u + w_ref[j:j + 1, :] * ext[H - shift:H - shift + L, :]
        ext[0:H, :] = raw[L - H:L, :]
        return u * jax.nn.sigmoid(u)

    q = conv_silu(q_ref, qext, wq_ref, bq_ref)
    k = conv_silu(k_ref, kext, wk_ref, bk_ref) * k_scale
    v = v_ref[0]
    qb = q.astype(BF16)
    kb = k.astype(BF16)

    i_col = gcol_ref[0, 0, :, 0:1]
    lf_col = jax.nn.log_sigmoid(gcol_ref[0, 0, :, 1:2])
    i_row = grow_ref[0, 0, 0:1, :]
    lf_row = jax.nn.log_sigmoid(grow_ref[0, 0, 1:2, :])

    t_idx = lax.broadcasted_iota(jnp.int32, (L, L), 0)
    s_idx = lax.broadcasted_iota(jnp.int32, (L, L), 1)
    tril = s_idx <= t_idx
    bcum_col = jnp.sum(jnp.where(tril, lf_row, 0.0), axis=1, keepdims=True)
    bcum_row = jnp.sum(jnp.where(t_idx <= s_idx, lf_col, 0.0), axis=0, keepdims=True)
    m_prev = m_ref[...]
    d = jnp.where(tril, bcum_col - bcum_row + i_row, -jnp.inf)
    m_inter = bcum_col + m_prev
    m_t = jnp.maximum(m_inter, jnp.max(d, axis=1, keepdims=True))
    w = jnp.exp(d - m_t)
    s_inter = jnp.exp(m_inter - m_t)

    ct = ct_ref[...]
    n_prev = n_ref[...]
    sc = _nt_dot(qb, kb) * w
    num = _dot(sc.astype(BF16), v) + s_inter * _dot(qb, ct.astype(BF16))
    den = jnp.sum(sc, axis=1, keepdims=True) + s_inter * jnp.sum(q * n_prev, axis=1, keepdims=True)
    h = num / jnp.maximum(jnp.abs(den), jnp.exp(-m_t))

    b_end = jnp.sum(lf_row, axis=1, keepdims=True)
    g_col = b_end - bcum_col + i_col
    m_new = jnp.maximum(b_end + m_prev, jnp.max(g_col, axis=0, keepdims=True))
    decay = jnp.exp(b_end + m_prev - m_new)
    wk = jnp.exp(g_col - m_new)
    vw = (v.astype(F32) * wk).astype(BF16)
    ct_ref[...] = decay * ct + _tn_dot(kb, vw)
    n_ref[...] = decay * n_prev + jnp.sum(k * wk, axis=0, keepdims=True)
    m_ref[...] = m_new

    hn = _rms(h, ng_ref[...])
    y_ref[0] = (hn * jax.nn.sigmoid(o_ref[0])).astype(y_ref.dtype)


def _mlstm(mlqk, mlv, mlo, gcol, grow, conv_w, conv_b, norm_g):
    b, s, w2 = mlqk.shape
    wd = w2 // 2
    nh = ML_HEADS
    dh = wd // nh
    L = ML_CHUNK
    blk = lambda off: pl.BlockSpec((1, L, dh), lambda bi, h, c: (bi, c, off + h))
    return pl.pallas_call(
        functools.partial(_mlstm_kernel, k_scale=dh ** -0.5),
        grid=(b, nh, s // L),
        in_specs=[
            blk(0), blk(nh), blk(0), blk(0),
            pl.BlockSpec((1, 1, L, 2), lambda bi, h, c: (bi, h, c, 0)),
            pl.BlockSpec((1, 1, 2, L), lambda bi, h, c: (bi, h, 0, c)),
            pl.BlockSpec((CONV_W, dh), lambda bi, h, c: (0, h)),
            pl.BlockSpec((CONV_W, dh), lambda bi, h, c: (0, nh + h)),
            pl.BlockSpec((1, dh), lambda bi, h, c: (0, h)),
            pl.BlockSpec((1, dh), lambda bi, h, c: (0, nh + h)),
            pl.BlockSpec((1, dh), lambda bi, h, c: (0, h)),
        ],
        out_specs=blk(0),
        out_shape=jax.ShapeDtypeStruct((b, s, wd), BF16),
        scratch_shapes=[
            pltpu.VMEM((L + CONV_HALO, dh), F32),
            pltpu.VMEM((L + CONV_HALO, dh), F32),
            pltpu.VMEM((dh, dh), F32),
            pltpu.VMEM((1, dh), F32),
            pltpu.VMEM((1, 1), F32),
        ],
        compiler_params=_params("parallel", "parallel", "arbitrary"),
        name="mlstm",
    )(mlqk, mlqk, mlv, mlo, gcol, grow, conv_w, conv_w, conv_b.reshape(1, w2), conv_b.reshape(1, w2),
      norm_g.reshape(1, wd))


def _kv_prep_kernel(kv_ref, gk_ref, kn_ref, mv_ref):
    xw = kn_ref.shape[1]
    dh = xw // X_HEADS
    for h in range(X_HEADS):
        kh = kv_ref[:, h * dh:(h + 1) * dh]
        kn_ref[:, h * dh:(h + 1) * dh] = _rms(kh, gk_ref[...]).astype(kn_ref.dtype)
    mv_ref[...] = kv_ref[:, xw:].astype(mv_ref.dtype)


def _kv_prep(kv, gk, tm):
    t, w2 = kv.shape
    xw = w2 // 2
    return pl.pallas_call(
        _kv_prep_kernel,
        grid=(t // tm,),
        in_specs=[pl.BlockSpec((tm, w2), lambda i: (i, 0)), pl.BlockSpec((1, xw // X_HEADS), lambda i: (0, 0))],
        out_specs=[pl.BlockSpec((tm, xw), lambda i: (i, 0)), pl.BlockSpec((tm, xw), lambda i: (i, 0))],
        out_shape=[jax.ShapeDtypeStruct((t, xw), BF16), jax.ShapeDtypeStruct((t, xw), BF16)],
        compiler_params=_params("parallel"),
        name="kv_prep",
    )(kv, gk.reshape(1, -1))


def _xattn_kernel(q_ref, kn_ref, mv_ref, o_ref):
    xw = q_ref.shape[2]
    dh = xw // X_HEADS
    for h in range(X_HEADS):
        sl = slice(h * dh, (h + 1) * dh)
        logits = _nt_dot(q_ref[0, :, sl], kn_ref[0, :, sl]) * (dh ** -0.5)
        e = jnp.exp(logits - jnp.max(logits, axis=-1, keepdims=True))
        p = e / jnp.sum(e, axis=-1, keepdims=True)
        o_ref[0, :, sl] = _dot(p.astype(BF16), mv_ref[0, :, sl]).astype(o_ref.dtype)


def _cross_attention(qn, kn, mv, tm):
    b, s, w = qn.shape
    m = kn.shape[1]
    return pl.pallas_call(
        _xattn_kernel,
        grid=(b, s // tm),
        in_specs=[
            pl.BlockSpec((1, tm, w), lambda bi, i: (bi, i, 0)),
            pl.BlockSpec((1, m, w), lambda bi, i: (bi, 0, 0)),
            pl.BlockSpec((1, m, w), lambda bi, i: (bi, 0, 0)),
        ],
        out_specs=pl.BlockSpec((1, tm, w), lambda bi, i: (bi, i, 0)),
        out_shape=jax.ShapeDtypeStruct((b, s, w), BF16),
        compiler_params=_params("parallel", "parallel"),
        name="cross_attention",
    )(qn, kn, mv)


def _merge_kernel(x_ref, g_ref, ysb_ref, yml_ref, yx_ref, wg_ref, bg_ref, wsb_ref, wml_ref, wx_ref, wo_ref, o_ref):
    d = x_ref.shape[1]
    h = _rms(x_ref[...], g_ref[...]).astype(BF16)
    mixed = None
    for i, (y_ref, w_ref) in enumerate(((ysb_ref, wsb_ref), (yml_ref, wml_ref), (yx_ref, wx_ref))):
        gate = jax.nn.sigmoid(_dot(h, wg_ref[:, i * d:(i + 1) * d]) + bg_ref[:, i * d:(i + 1) * d])
        term = gate * _dot(y_ref[...], w_ref[...])
        mixed = term if mixed is None else mixed + term
    o_ref[...] = x_ref[...] + _dot(mixed.astype(BF16), wo_ref[...])


def _merge(x2d, g, ysb, yml, yx, wg, bg, wsb, wml, wx, wo, tm):
    t, d = x2d.shape
    row = lambda: pl.BlockSpec((tm, d), lambda i: (i, 0))
    const = lambda shape: pl.BlockSpec(shape, lambda i: (0, 0), pipeline_mode=pl.Buffered(1))
    return pl.pallas_call(
        _merge_kernel,
        grid=(t // tm,),
        in_specs=[row(), const((1, d)), row(), row(), row(), const((d, N_BRANCH * d)), const((1, N_BRANCH * d)),
                  const((d, d)), const((d, d)), const((d, d)), const((d, d))],
        out_specs=row(),
        out_shape=jax.ShapeDtypeStruct((t, d), F32),
        compiler_params=_params("parallel"),
        name="merge_out",
    )(x2d, g.reshape(1, d), ysb, yml, yx, wg, bg.reshape(1, -1), wsb, wml, wx, wo)


def _mlp_kernel(x_ref, g_ref, w1_ref, w2_ref, o_ref, *, ff_chunk):
    x = x_ref[...]
    hb = _rms(x, g_ref[...]).astype(BF16)
    acc = x
    for c in range(w1_ref.shape[1] // ff_chunk):
        u = _dot(hb, w1_ref[:, c * ff_chunk:(c + 1) * ff_chunk])
        a = jnp.square(jnp.maximum(u, 0.0)).astype(BF16)
        acc = acc + _dot(a, w2_ref[c * ff_chunk:(c + 1) * ff_chunk, :])
    o_ref[...] = acc


def _mlp(x2d, g, w1, w2, tm, ff_chunk):
    t, d = x2d.shape
    ff = w1.shape[1]
    const = lambda shape: pl.BlockSpec(shape, lambda i: (0, 0), pipeline_mode=pl.Buffered(1))
    return pl.pallas_call(
        functools.partial(_mlp_kernel, ff_chunk=ff_chunk),
        grid=(t // tm,),
        in_specs=[pl.BlockSpec((tm, d), lambda i: (i, 0)), const((1, d)), const((d, ff)), const((ff, d))],
        out_specs=pl.BlockSpec((tm, d), lambda i: (i, 0)),
        out_shape=jax.ShapeDtypeStruct((t, d), F32),
        compiler_params=_params("parallel"),
        name="mlp",
    )(x2d, g.reshape(1, d), w1, w2)


def _layer(x, mem, g_mix, w_in, b_if, b_gate, conv_w, conv_b, ml_norm_g, g_mem, w_mem_kv, q_norm_g, k_norm_g,
           w_sb_proj, w_ml_proj, w_x_proj, w_out, g_mlp, w_ff1, w_ff2):
    b, s, d = x.shape
    m = mem.shape[1]
    t = b * s
    sb_w = w_sb_proj.shape[0]
    ml_w = w_ml_proj.shape[0]
    x_w = w_x_proj.shape[0]
    sb_heads = sb_w // SB_HD
    tm = min(512, t)

    sizes = (sb_w, sb_w, sb_w, ml_w, ml_w, ml_w, ml_w, ML_HEADS, ML_HEADS, x_w, N_BRANCH * d)
    offs = [0]
    for sz in sizes:
        offs.append(offs[-1] + sz)
    col = lambda a, bnd: w_in[:, offs[a]:offs[bnd]]

    x2d = x.reshape(t, d)
    gif_w = 128
    w_cat = jnp.concatenate([col(0, 7), col(9, 10), jnp.pad(col(7, 9), ((0, 0), (0, gif_w - 2 * ML_HEADS)))],
                            axis=1).astype(BF16)
    qkv, mlqk, mlv, mlo, gif, qn = _inproj(x2d, g_mix, w_cat, q_norm_g,
                                           (3 * sb_w, 2 * ml_w, ml_w, ml_w, x_w, gif_w), tm)
    gates_if = gif[:, :2 * ML_HEADS] + b_if

    y_sb = _sb_attention(qkv.reshape(b, s, 3 * sb_w), sb_heads).reshape(t, sb_w)

    gif = gates_if.reshape(b, s, 2, ML_HEADS)
    gcol = gif.transpose(0, 3, 1, 2)
    grow = gif.transpose(0, 3, 2, 1)
    y_ml = _mlstm(mlqk.reshape(b, s, 2 * ml_w), mlv.reshape(b, s, ml_w), mlo.reshape(b, s, ml_w),
                  gcol, grow, conv_w, conv_b, ml_norm_g).reshape(t, ml_w)

    tmm = min(512, b * m)
    hm = _norm_cast(mem.reshape(b * m, d), g_mem, tmm)
    kv = _matmul(hm, w_mem_kv.astype(BF16), F32, tmm, min(1024, 2 * x_w), "proj_mem_kv")
    kn, mv = _kv_prep(kv, k_norm_g, tmm)
    y_x = _cross_attention(qn.reshape(b, s, x_w), kn.reshape(b, m, x_w), mv.reshape(b, m, x_w),
                           min(512, s)).reshape(t, x_w)

    x1 = _merge(x2d, g_mix, y_sb, y_ml, y_x, col(10, 11).astype(BF16), b_gate, w_sb_proj.astype(BF16),
                w_ml_proj.astype(BF16), w_x_proj.astype(BF16), w_out.astype(BF16), min(256, t))
    x2 = _mlp(x1, g_mlp, w_ff1.astype(BF16), w_ff2.astype(BF16), tm, min(1024, w_ff1.shape[1]))
    return x2.reshape(b, s, d)


def kernel(x, mem, g_mix, w_in, b_if, b_gate, conv_w, conv_b, ml_norm_g, g_mem, w_mem_kv, q_norm_g, k_norm_g,
           w_sb_proj, w_ml_proj, w_x_proj, w_out, g_mlp, w_ff1, w_ff2):
    for l in range(g_mix.shape[0]):
        x = _layer(x, mem, g_mix[l], w_in[l], b_if[l], b_gate[l], conv_w[l], conv_b[l], ml_norm_g[l], g_mem[l],
                   w_mem_kv[l], q_norm_g[l], k_norm_g[l], w_sb_proj[l], w_ml_proj[l], w_x_proj[l], w_out[l],
                   g_mlp[l], w_ff1[l], w_ff2[l])
    return x
```

```python
import functools

import jax
import jax.numpy as jnp
from jax import lax
from jax.experimental import pallas as pl
from jax.experimental.pallas import tpu as pltpu

EPS = 1e-6
LOG2E = 1.4426950408889634
SB_HD = 128
ML_HEADS = 4
X_HEADS = 4
CONV_W = 4
N_BRANCH = 3
SB_BLOCK = 128
SB_QBLOCKS = 8
SB_KBLOCKS = 4
INPROJ_COLS = 1024
ML_CHUNK = 256
CONV_PITCH = 4
SUBLANES = 8
LANES = 128
VMEM_LIMIT = 56 * 1024 * 1024

F32 = jnp.float32
BF16 = jnp.bfloat16


def _params(*sem):
    return pltpu.CompilerParams(dimension_semantics=sem, vmem_limit_bytes=VMEM_LIMIT)


def _nt_dot(a, b):
    return lax.dot_general(a, b, (((1,), (1,)), ((), ())), preferred_element_type=F32)


def _tn_dot(a, b):
    return lax.dot_general(a, b, (((0,), (0,)), ((), ())), preferred_element_type=F32)


def _dot(a, b):
    return jnp.dot(a, b, preferred_element_type=F32)


def _rms(x, g):
    return x * lax.rsqrt(jnp.mean(x * x, axis=-1, keepdims=True) + EPS) * g


def _norm_kernel(x_ref, g_ref, o_ref):
    o_ref[...] = _rms(x_ref[...], g_ref[...]).astype(o_ref.dtype)


def _norm_cast(x2d, g, tm):
    t, d = x2d.shape
    return pl.pallas_call(
        _norm_kernel,
        grid=(t // tm,),
        in_specs=[pl.BlockSpec((tm, d), lambda i: (i, 0)), pl.BlockSpec((1, d), lambda i: (0, 0))],
        out_specs=pl.BlockSpec((tm, d), lambda i: (i, 0)),
        out_shape=jax.ShapeDtypeStruct((t, d), BF16),
        compiler_params=_params("parallel"),
        name="norm_cast",
    )(x2d, g.reshape(1, d))


def _matmul_kernel(a_ref, w_ref, o_ref):
    o_ref[...] = _dot(a_ref[...], w_ref[...]).astype(o_ref.dtype)


def _matmul(a, w, out_dtype, tm, tn, name):
    t, k = a.shape
    n = w.shape[1]
    return pl.pallas_call(
        _matmul_kernel,
        grid=(n // tn, t // tm),
        in_specs=[pl.BlockSpec((tm, k), lambda j, i: (i, 0)), pl.BlockSpec((k, tn), lambda j, i: (0, j))],
        out_specs=pl.BlockSpec((tm, tn), lambda j, i: (i, j)),
        out_shape=jax.ShapeDtypeStruct((t, n), out_dtype),
        compiler_params=_params("parallel", "parallel"),
        name=name,
    )(a, w)


def _inproj_kernel(x_ref, g_ref, w_ref, wt_ref, gq_ref, qkv_ref, mlqk_ref, mlv_ref, mlo_ref, gif_ref, gift_ref, qn_ref,
                   *, widths):
    hb = _rms(x_ref[...], g_ref[...]).astype(BF16)
    sb3, ml2, mlv_w, mlo_w, xq_w, _ = widths
    step = INPROJ_COLS

    def proj(c0, c1):
        return _dot(hb, w_ref[:, c0:c1])

    def emit(out_ref, base, width):
        for c in range(0, width, step):
            out_ref[:, c:c + step] = proj(base + c, base + c + step).astype(out_ref.dtype)

    emit(qkv_ref, 0, sb3)
    emit(mlqk_ref, sb3, ml2)
    emit(mlv_ref, sb3 + ml2, mlv_w)
    emit(mlo_ref, sb3 + ml2 + mlv_w, mlo_w)
    base = sb3 + ml2 + mlv_w + mlo_w
    dh = xq_w // X_HEADS
    for h in range(X_HEADS):
        qn_ref[:, h * dh:(h + 1) * dh] = _rms(proj(base + h * dh, base + (h + 1) * dh), gq_ref[...]).astype(BF16)
    gif_ref[...] = proj(base + xq_w, base + xq_w + gif_ref.shape[1])
    gift_ref[...] = _nt_dot(wt_ref[...], hb)[:gift_ref.shape[0]]


def _inproj(x2d, g, w_cat, w_gif_t, gq, widths, tm):
    t, d = x2d.shape
    sb3, ml2, mlv_w, mlo_w, xq_w, gif_w = widths
    row = lambda n: pl.BlockSpec((tm, n), lambda i: (i, 0))
    const = lambda shape: pl.BlockSpec(shape, lambda i: (0, 0), pipeline_mode=pl.Buffered(1))
    return pl.pallas_call(
        functools.partial(_inproj_kernel, widths=widths),
        grid=(t // tm,),
        in_specs=[row(d), const((1, d)), const(w_cat.shape), const(w_gif_t.shape), const((1, xq_w // X_HEADS))],
        out_specs=[row(sb3), row(ml2), row(mlv_w), row(mlo_w), row(gif_w),
                   pl.BlockSpec((SUBLANES, tm), lambda i: (0, i)), row(xq_w)],
        out_shape=[jax.ShapeDtypeStruct((t, sb3), BF16), jax.ShapeDtypeStruct((t, ml2), F32),
                   jax.ShapeDtypeStruct((t, mlv_w), BF16), jax.ShapeDtypeStruct((t, mlo_w), F32),
                   jax.ShapeDtypeStruct((t, gif_w), F32), jax.ShapeDtypeStruct((SUBLANES, t), F32),
                   jax.ShapeDtypeStruct((t, xq_w), BF16)],
        compiler_params=_params("parallel"),
        name="in_proj",
    )(x2d, g.reshape(1, d), w_cat, w_gif_t, gq.reshape(1, -1))


def _sb_kernel(q_ref, k_ref, v_ref, tri_ref, o_ref, acc_ref, carry_ref, z_ref, a_ref, *, scale):
    blk = SB_BLOCK
    tq = blk * SB_QBLOCKS
    tk = blk * SB_KBLOCKS
    qi = pl.program_id(2)

    def key_start(g):
        return pl.multiple_of(g * tk, tk)

    def mask_top(x):
        causal = (lax.broadcasted_iota(jnp.int32, (blk, blk), 1) < lax.broadcasted_iota(jnp.int32, (blk, blk), 0))
        top = jnp.where(causal, x[:blk], 0.0)
        return top if x.shape[0] == blk else jnp.concatenate([top, x[blk:]], axis=0)

    def logits(g, r_base=0):
        return _nt_dot(q_ref[0, r_base:, :], k_ref[0, pl.ds(key_start(g), tk), :])

    def weights(z_all, r_base, diag):
        a_blocks = [None] * SB_KBLOCKS
        for j in reversed(range(SB_KBLOCKS)):
            r0 = j * blk if diag else 0
            z = z_all[r0:, j * blk:(j + 1) * blk] * (scale * LOG2E)
            l = jnp.log(1.0 + jnp.exp2(-jnp.abs(z))) * LOG2E
            sp = jnp.maximum(z, 0.0) + l
            if diag:
                sp = mask_top(sp)
            hi = sp.astype(BF16)
            lo = (sp - hi.astype(F32)).astype(BF16)
            r = _dot(jnp.concatenate([hi, lo], axis=1), tri_ref[...])
            carry = carry_ref[r_base + r0:, :]
            a = jnp.exp2(z - r[:, :blk] - carry)
            if diag:
                a = mask_top(a)
            carry_ref[r_base + r0:, :] = carry + r[:, blk:]
            a = a.astype(BF16)
            a_blocks[j] = a if r0 == 0 else jnp.concatenate([jnp.zeros((r0, blk), BF16), a], axis=0)
        return jnp.concatenate(a_blocks, axis=1)

    def accumulate(a, g, r_base=0):
        acc_ref[r_base:, :] += _dot(a, v_ref[0, pl.ds(key_start(g), tk), :])

    carry_ref[...] = jnp.zeros_like(carry_ref)
    acc_ref[...] = jnp.zeros_like(acc_ref)
    n_diag = SB_QBLOCKS // SB_KBLOCKS
    assert n_diag * SB_KBLOCKS == SB_QBLOCKS and n_diag % 2 == 0
    n_full = qi * n_diag
    z_next = logits(n_full + n_diag - 1, (n_diag - 1) * tk)
    for d in reversed(range(n_diag)):
        z_cur = z_next
        if d > 0:
            z_next = logits(n_full + d - 1, (d - 1) * tk)
        else:
            z_ref[...] = logits(jnp.maximum(n_full - 1, 0))
        a = weights(z_cur, d * tk, True)
        if d > 0:
            accumulate(a, n_full + d, d * tk)
        else:
            a_ref[...] = a

    def body(i, c):
        ga = n_full - 1 - 2 * i
        gb = ga - 1
        z_b = logits(gb)
        accumulate(a_ref[...], ga + 1)
        a_a = weights(z_ref[...], 0, False)
        z_ref[...] = logits(jnp.maximum(ga - 2, 0))
        accumulate(a_a, ga)
        a_ref[...] = weights(z_b, 0, False)
        return c

    lax.fori_loop(0, n_full // 2, body, 0)
    accumulate(a_ref[...], 0)
    o_ref[0] = acc_ref[...].astype(o_ref.dtype)


def _sb_attention(qkv, n_heads):
    b, s, _ = qkv.shape
    blk = SB_BLOCK
    tq = blk * SB_QBLOCKS
    assert s % tq == 0
    row = lax.broadcasted_iota(jnp.int32, (2 * blk, 2 * blk), 0) % blk
    col = lax.broadcasted_iota(jnp.int32, (2 * blk, 2 * blk), 1)
    tri = jnp.where((col >= blk) | (row >= col), 1.0, 0.0).astype(BF16)
    return pl.pallas_call(
        functools.partial(_sb_kernel, scale=SB_HD ** -0.5),
        grid=(b, n_heads, s // tq),
        in_specs=[
            pl.BlockSpec((1, tq, SB_HD), lambda bi, h, i: (bi, i, h)),
            pl.BlockSpec((1, s, SB_HD), lambda bi, h, i: (bi, 0, n_heads + h)),
            pl.BlockSpec((1, s, SB_HD), lambda bi, h, i: (bi, 0, 2 * n_heads + h)),
            pl.BlockSpec((2 * blk, 2 * blk), lambda bi, h, i: (0, 0)),
        ],
        out_specs=pl.BlockSpec((1, tq, SB_HD), lambda bi, h, i: (bi, i, h)),
        out_shape=jax.ShapeDtypeStruct((b, s, n_heads * SB_HD), BF16),
        scratch_shapes=[pltpu.VMEM((tq, SB_HD), F32), pltpu.VMEM((tq, blk), F32),
                        pltpu.VMEM((tq, blk * SB_KBLOCKS), F32), pltpu.VMEM((tq, blk * SB_KBLOCKS), BF16)],
        compiler_params=_params("parallel", "parallel", "arbitrary"),
        name="sb_attention",
    )(qkv, qkv, qkv, tri)


def _silu(u):
    half = 0.5 * u
    return half + half * jnp.tanh(half)


def _conv_silu(raw_refs, w_ref, b_ref, lane0, halo_ref, out_ref, out_scale):
    P = CONV_PITCH
    rows = SUBLANES * P
    first = lax.broadcasted_iota(jnp.int32, (SUBLANES, LANES), 0) == 0
    for half, raw_ref in enumerate(raw_refs):
        lanes = slice(lane0 + half * LANES, lane0 + (half + 1) * LANES)
        w = [jnp.broadcast_to(w_ref[j:j + 1, lanes], (SUBLANES, LANES)) for j in range(CONV_W)]
        bias = jnp.broadcast_to(b_ref[:, lanes], (SUBLANES, LANES))
        tail = [halo_ref[half, m] for m in range(CONV_W - 1)]
        for blk in range(raw_ref.shape[1] // rows):
            base = blk * rows
            x = [raw_ref[0, pl.ds(base + i, SUBLANES, stride=P), :] for i in range(P)]
            moved = [pltpu.roll(x[P - (CONV_W - 1) + m], 1, axis=0) for m in range(CONV_W - 1)]
            virt = [jnp.where(first, tail[m], moved[m]) for m in range(CONV_W - 1)]
            ext = virt + x
            for i in range(P):
                u = bias
                for j in range(CONV_W):
                    u = u + w[j] * ext[i + j]
                y = _silu(u)
                out_ref[half, pl.ds(base + i, SUBLANES, stride=P), :] = y if out_scale == 1.0 else y * out_scale
            tail = moved
        for m in range(CONV_W - 1):
            halo_ref[half, m] = tail[m]


def _mlstm_kernel(*refs, k_scale, dh):
    nh = ML_HEADS
    L = ML_CHUNK
    bif_ref = refs[0]
    q_slabs, k_slabs = refs[1:1 + 2 * nh], refs[1 + 2 * nh:1 + 4 * nh]
    v_ref, o_ref, gcol_ref, grow_ref, cw_ref, cb_ref, ng_ref, y_ref = refs[1 + 4 * nh:9 + 4 * nh]
    qhalo, khalo, qs_ref, ks_ref, ct_ref, n_ref, m_ref = refs[9 + 4 * nh:]
    c = pl.program_id(1)
    heads = range(nh)

    @pl.when(c == 0)
    def _():
        qhalo[...] = jnp.zeros_like(qhalo)
        khalo[...] = jnp.zeros_like(khalo)
        ct_ref[...] = jnp.zeros_like(ct_ref)
        n_ref[...] = jnp.zeros_like(n_ref)
        m_ref[...] = jnp.zeros_like(m_ref)

    col = lambda h: slice(h * dh, (h + 1) * dh)
    for h in heads:
        _conv_silu(q_slabs[2 * h:2 * h + 2], cw_ref, cb_ref, h * dh, qhalo.at[h], qs_ref.at[h], 1.0)
        _conv_silu(k_slabs[2 * h:2 * h + 2], cw_ref, cb_ref, (nh + h) * dh, khalo.at[h], ks_ref.at[h], k_scale)

    t_idx = lax.broadcasted_iota(jnp.int32, (L, L), 0)
    s_idx = lax.broadcasted_iota(jnp.int32, (L, L), 1)
    tril = s_idx <= t_idx
    triu = t_idx <= s_idx

    gate = []
    for h in heads:
        b_i = bif_ref[h]
        b_f = bif_ref[nh + h]
        i_col = gcol_ref[0, :, h:h + 1] + b_i
        lf_col = jax.nn.log_sigmoid(gcol_ref[0, :, nh + h:nh + h + 1] + b_f)
        i_row = grow_ref[h:h + 1, :] + b_i
        lf_row = jax.nn.log_sigmoid(grow_ref[nh + h:nh + h + 1, :] + b_f)
        bcum_col = jnp.sum(jnp.where(tril, lf_row, 0.0), axis=1, keepdims=True)
        bcum_row = jnp.sum(jnp.where(triu, lf_col, 0.0), axis=0, keepdims=True)
        m_prev = m_ref[h]
        d = jnp.where(tril, bcum_col - bcum_row + i_row, -jnp.inf)
        m_inter = bcum_col + m_prev
        m_t = jnp.maximum(m_inter, jnp.max(d, axis=1, keepdims=True))
        b_end = jnp.sum(lf_row, axis=1, keepdims=True)
        g_col = b_end - bcum_col + i_col
        m_new = jnp.maximum(b_end + m_prev, jnp.max(g_col, axis=0, keepdims=True))
        gate.append(dict(w=jnp.exp(d - m_t), s_inter=jnp.exp(m_inter - m_t), floor=jnp.exp(-m_t),
                         decay=jnp.exp(b_end + m_prev - m_new), wk=jnp.exp(g_col - m_new), m_new=m_new))

    q = [jnp.concatenate([qs_ref[h, 0], qs_ref[h, 1]], axis=1) for h in heads]
    k = [jnp.concatenate([ks_ref[h, 0], ks_ref[h, 1]], axis=1) for h in heads]
    qb = [x.astype(BF16) for x in q]
    kb = [x.astype(BF16) for x in k]
    qk = [_nt_dot(qb[h], kb[h]) for h in heads]
    inter = [_dot(qb[h], ct_ref[h].astype(BF16)) for h in heads]

    for h in heads:
        g = gate[h]
        v = v_ref[0, :, col(h)]
        sc = qk[h] * g["w"]
        num = _dot(sc.astype(BF16), v) + g["s_inter"] * inter[h]
        den = jnp.sum(sc, axis=1, keepdims=True) + g["s_inter"] * jnp.sum(q[h] * n_ref[h], axis=1, keepdims=True)
        hh = num / jnp.maximum(jnp.abs(den), g["floor"])
        hn = _rms(hh, ng_ref[:, col(h)])
        y_ref[0, :, col(h)] = (hn * (0.5 + 0.5 * jnp.tanh(0.5 * o_ref[0, :, col(h)]))).astype(y_ref.dtype)

    for h in heads:
        g = gate[h]
        vw = (v_ref[0, :, col(h)].astype(F32) * g["wk"]).astype(BF16)
        ct_ref[h] = g["decay"] * ct_ref[h] + _tn_dot(kb[h], vw)
        n_ref[h] = g["decay"] * n_ref[h] + jnp.sum(k[h] * g["wk"], axis=0, keepdims=True)
        m_ref[h] = g["m_new"]


def _mlstm(mlqk, mlv, mlo, gif, gif_t, b_if, conv_w, conv_b, norm_g):
    b, s, w2 = mlqk.shape
    wd = w2 // 2
    nh = ML_HEADS
    dh = wd // nh
    assert dh == 2 * LANES
    L = ML_CHUNK
    nc = s // L
    slab = lambda j: pl.BlockSpec((1, L, LANES), lambda bi, c: (bi, c, j))
    full = lambda n: pl.BlockSpec((1, L, n), lambda bi, c: (bi, c, 0))
    const = lambda shape: pl.BlockSpec(shape, lambda bi, c: (0, 0))
    n_slabs = w2 // LANES
    halo = pltpu.VMEM((nh, 2, CONV_W - 1, SUBLANES, LANES), F32)
    slabs = pltpu.VMEM((nh, 2, L, LANES), F32)
    return pl.pallas_call(
        functools.partial(_mlstm_kernel, k_scale=dh ** -0.5, dh=dh),
        grid=(b, nc),
        in_specs=[pl.BlockSpec(memory_space=pltpu.SMEM)] + [slab(j) for j in range(n_slabs)] + [
            full(wd), full(wd), full(LANES),
            pl.BlockSpec((SUBLANES, L), lambda bi, c: (0, bi * nc + c)),
            const((CONV_W, w2)), const((1, w2)), const((1, wd)),
        ],
        out_specs=full(wd),
        out_shape=jax.ShapeDtypeStruct((b, s, wd), BF16),
        scratch_shapes=[halo, halo, slabs, slabs, pltpu.VMEM((nh, dh, dh), F32), pltpu.VMEM((nh, 1, dh), F32),
                        pltpu.VMEM((nh, 1, 1), F32)],
        compiler_params=_params("parallel", "arbitrary"),
        name="mlstm",
    )(b_if, *([mlqk] * n_slabs), mlv, mlo, gif, gif_t, conv_w, conv_b.reshape(1, w2), norm_g.reshape(1, wd))


def _kv_prep_kernel(kv_ref, gk_ref, kn_ref, mv_ref):
    xw = kn_ref.shape[1]
    dh = xw // X_HEADS
    for h in range(X_HEADS):
        kh = kv_ref[:, h * dh:(h + 1) * dh]
        kn_ref[:, h * dh:(h + 1) * dh] = _rms(kh, gk_ref[...]).astype(kn_ref.dtype)
    mv_ref[...] = kv_ref[:, xw:].astype(mv_ref.dtype)


def _kv_prep(kv, gk, tm):
    t, w2 = kv.shape
    xw = w2 // 2
    return pl.pallas_call(
        _kv_prep_kernel,
        grid=(t // tm,),
        in_specs=[pl.BlockSpec((tm, w2), lambda i: (i, 0)), pl.BlockSpec((1, xw // X_HEADS), lambda i: (0, 0))],
        out_specs=[pl.BlockSpec((tm, xw), lambda i: (i, 0)), pl.BlockSpec((tm, xw), lambda i: (i, 0))],
        out_shape=[jax.ShapeDtypeStruct((t, xw), BF16), jax.ShapeDtypeStruct((t, xw), BF16)],
        compiler_params=_params("parallel"),
        name="kv_prep",
    )(kv, gk.reshape(1, -1))


def _xattn_kernel(q_ref, kn_ref, mv_ref, o_ref):
    xw = q_ref.shape[2]
    dh = xw // X_HEADS
    for h in range(X_HEADS):
        sl = slice(h * dh, (h + 1) * dh)
        logits = _nt_dot(q_ref[0, :, sl], kn_ref[0, :, sl]) * (dh ** -0.5)
        e = jnp.exp(logits - jnp.max(logits, axis=-1, keepdims=True))
        p = e / jnp.sum(e, axis=-1, keepdims=True)
        o_ref[0, :, sl] = _dot(p.astype(BF16), mv_ref[0, :, sl]).astype(o_ref.dtype)


def _cross_attention(qn, kn, mv, tm):
    b, s, w = qn.shape
    m = kn.shape[1]
    return pl.pallas_call(
        _xattn_kernel,
        grid=(b, s // tm),
        in_specs=[
            pl.BlockSpec((1, tm, w), lambda bi, i: (bi, i, 0)),
            pl.BlockSpec((1, m, w), lambda bi, i: (bi, 0, 0)),
            pl.BlockSpec((1, m, w), lambda bi, i: (bi, 0, 0)),
        ],
        out_specs=pl.BlockSpec((1, tm, w), lambda bi, i: (bi, i, 0)),
        out_shape=jax.ShapeDtypeStruct((b, s, w), BF16),
        compiler_params=_params("parallel", "parallel"),
        name="cross_attention",
    )(qn, kn, mv)


def _merge_kernel(x_ref, g_ref, ysb_ref, yml_ref, yx_ref, wg_ref, bg_ref, wsb_ref, wml_ref, wx_ref, wo_ref, o_ref):
    d = x_ref.shape[1]
    h = _rms(x_ref[...], g_ref[...]).astype(BF16)
    mixed = None
    for i, (y_ref, w_ref) in enumerate(((ysb_ref, wsb_ref), (yml_ref, wml_ref), (yx_ref, wx_ref))):
        gate = jax.nn.sigmoid(_dot(h, wg_ref[:, i * d:(i + 1) * d]) + bg_ref[:, i * d:(i + 1) * d])
        term = gate * _dot(y_ref[...], w_ref[...])
        mixed = term if mixed is None else mixed + term
    o_ref[...] = x_ref[...] + _dot(mixed.astype(BF16), wo_ref[...])


def _merge(x2d, g, ysb, yml, yx, wg, bg, wsb, wml, wx, wo, tm):
    t, d = x2d.shape
    row = lambda: pl.BlockSpec((tm, d), lambda i: (i, 0))
    const = lambda shape: pl.BlockSpec(shape, lambda i: (0, 0), pipeline_mode=pl.Buffered(1))
    return pl.pallas_call(
        _merge_kernel,
        grid=(t // tm,),
        in_specs=[row(), const((1, d)), row(), row(), row(), const((d, N_BRANCH * d)), const((1, N_BRANCH * d)),
                  const((d, d)), const((d, d)), const((d, d)), const((d, d))],
        out_specs=row(),
        out_shape=jax.ShapeDtypeStruct((t, d), F32),
        compiler_params=_params("parallel"),
        name="merge_out",
    )(x2d, g.reshape(1, d), ysb, yml, yx, wg, bg.reshape(1, -1), wsb, wml, wx, wo)


def _mlp_kernel(x_ref, g_ref, w1_ref, w2_ref, o_ref, *, ff_chunk):
    x = x_ref[...]
    hb = _rms(x, g_ref[...]).astype(BF16)
    acc = x
    for c in range(w1_ref.shape[1] // ff_chunk):
        u = _dot(hb, w1_ref[:, c * ff_chunk:(c + 1) * ff_chunk])
        a = jnp.square(jnp.maximum(u, 0.0)).astype(BF16)
        acc = acc + _dot(a, w2_ref[c * ff_chunk:(c + 1) * ff_chunk, :])
    o_ref[...] = acc


def _mlp(x2d, g, w1, w2, tm, ff_chunk):
    t, d = x2d.shape
    ff = w1.shape[1]
    const = lambda shape: pl.BlockSpec(shape, lambda i: (0, 0), pipeline_mode=pl.Buffered(1))
    return pl.pallas_call(
        functools.partial(_mlp_kernel, ff_chunk=ff_chunk),
        grid=(t // tm,),
        in_specs=[pl.BlockSpec((tm, d), lambda i: (i, 0)), const((1, d)), const((d, ff)), const((ff, d))],
        out_specs=pl.BlockSpec((tm, d), lambda i: (i, 0)),
        out_shape=jax.ShapeDtypeStruct((t, d), F32),
        compiler_params=_params("parallel"),
        name="mlp",
    )(x2d, g.reshape(1, d), w1, w2)


def _layer(x, mem, g_mix, w_in, b_if, b_gate, conv_w, conv_b, ml_norm_g, g_mem, w_mem_kv, q_norm_g, k_norm_g,
           w_sb_proj, w_ml_proj, w_x_proj, w_out, g_mlp, w_ff1, w_ff2):
    b, s, d = x.shape
    m = mem.shape[1]
    t = b * s
    sb_w = w_sb_proj.shape[0]
    ml_w = w_ml_proj.shape[0]
    x_w = w_x_proj.shape[0]
    sb_heads = sb_w // SB_HD
    tm = min(512, t)

    sizes = (sb_w, sb_w, sb_w, ml_w, ml_w, ml_w, ml_w, ML_HEADS, ML_HEADS, x_w, N_BRANCH * d)
    offs = [0]
    for sz in sizes:
        offs.append(offs[-1] + sz)
    col = lambda a, bnd: w_in[:, offs[a]:offs[bnd]]

    x2d = x.reshape(t, d)
    assert 2 * ML_HEADS == SUBLANES
    w_gif = jnp.pad(col(7, 9), ((0, 0), (0, LANES - 2 * ML_HEADS))).astype(BF16)
    w_cat = jnp.concatenate([col(0, 7).astype(BF16), col(9, 10).astype(BF16), w_gif], axis=1)
    qkv, mlqk, mlv, mlo, gif, gif_t, qn = _inproj(x2d, g_mix, w_cat, w_gif.T, q_norm_g,
                                                  (3 * sb_w, 2 * ml_w, ml_w, ml_w, x_w, LANES), tm)

    y_sb = _sb_attention(qkv.reshape(b, s, 3 * sb_w), sb_heads).reshape(t, sb_w)

    y_ml = _mlstm(mlqk.reshape(b, s, 2 * ml_w), mlv.reshape(b, s, ml_w), mlo.reshape(b, s, ml_w),
                  gif.reshape(b, s, LANES), gif_t, b_if, conv_w, conv_b, ml_norm_g).reshape(t, ml_w)

    tmm = min(512, b * m)
    hm = _norm_cast(mem.reshape(b * m, d), g_mem, tmm)
    kv = _matmul(hm, w_mem_kv.astype(BF16), F32, tmm, min(1024, 2 * x_w), "proj_mem_kv")
    kn, mv = _kv_prep(kv, k_norm_g, tmm)
    y_x = _cross_attention(qn.reshape(b, s, x_w), kn.reshape(b, m, x_w), mv.reshape(b, m, x_w),
                           min(512, s)).reshape(t, x_w)

    x1 = _merge(x2d, g_mix, y_sb, y_ml, y_x, col(10, 11).astype(BF16), b_gate, w_sb_proj.astype(BF16),
                w_ml_proj.astype(BF16), w_x_proj.astype(BF16), w_out.astype(BF16), tm)
    x2 = _mlp(x1, g_mlp, w_ff1.astype(BF16), w_ff2.astype(BF16), tm, min(1024, w_ff1.shape[1]))
    return x2.reshape(b, s, d)


def kernel(x, mem, g_mix, w_in, b_if, b_gate, conv_w, conv_b, ml_norm_g, g_mem, w_mem_kv, q_norm_g, k_norm_g,
           w_sb_proj, w_ml_proj, w_x_proj, w_out, g_mlp, w_ff1, w_ff2):
    for l in range(g_mix.shape[0]):
        x = _layer(x, mem, g_mix[l], w_in[l], b_if[l], b_gate[l], conv_w[l], conv_b[l], ml_norm_g[l], g_mem[l],
                   w_mem_kv[l], q_norm_g[l], k_norm_g[l], w_sb_proj[l], w_ml_proj[l], w_x_proj[l], w_out[l],
                   g_mlp[l], w_ff1[l], w_ff2[l])
    return x
```

```python
import functools

import jax
import jax.numpy as jnp
from jax import lax
from jax.experimental import pallas as pl
from jax.experimental.pallas import tpu as pltpu

EPS = 1e-6
LOG2E = 1.4426950408889634
SB_HD = 128
ML_HEADS = 4
X_HEADS = 4
CONV_W = 4
N_BRANCH = 3
SB_BLOCK = 128
SB_QBLOCKS = 16
SB_KBLOCKS = 4
INPROJ_COLS = 1024
ML_CHUNK = 256
CONV_PITCH = 4
SUBLANES = 8
LANES = 128
VMEM_LIMIT = 56 * 1024 * 1024

F32 = jnp.float32
BF16 = jnp.bfloat16


def _params(*sem):
    return pltpu.CompilerParams(dimension_semantics=sem, vmem_limit_bytes=VMEM_LIMIT)


def _nt_dot(a, b):
    return lax.dot_general(a, b, (((1,), (1,)), ((), ())), preferred_element_type=F32)


def _tn_dot(a, b):
    return lax.dot_general(a, b, (((0,), (0,)), ((), ())), preferred_element_type=F32)


def _dot(a, b):
    return jnp.dot(a, b, preferred_element_type=F32)


def _rms(x, g):
    return x * lax.rsqrt(jnp.mean(x * x, axis=-1, keepdims=True) + EPS) * g


def _norm_kernel(x_ref, g_ref, o_ref):
    o_ref[...] = _rms(x_ref[...], g_ref[...]).astype(o_ref.dtype)


def _norm_cast(x2d, g, tm):
    t, d = x2d.shape
    return pl.pallas_call(
        _norm_kernel,
        grid=(t // tm,),
        in_specs=[pl.BlockSpec((tm, d), lambda i: (i, 0)), pl.BlockSpec((1, d), lambda i: (0, 0))],
        out_specs=pl.BlockSpec((tm, d), lambda i: (i, 0)),
        out_shape=jax.ShapeDtypeStruct((t, d), BF16),
        compiler_params=_params("parallel"),
        name="norm_cast",
    )(x2d, g.reshape(1, d))


def _matmul_kernel(a_ref, w_ref, o_ref):
    o_ref[...] = _dot(a_ref[...], w_ref[...]).astype(o_ref.dtype)


def _matmul(a, w, out_dtype, tm, tn, name):
    t, k = a.shape
    n = w.shape[1]
    return pl.pallas_call(
        _matmul_kernel,
        grid=(n // tn, t // tm),
        in_specs=[pl.BlockSpec((tm, k), lambda j, i: (i, 0)), pl.BlockSpec((k, tn), lambda j, i: (0, j))],
        out_specs=pl.BlockSpec((tm, tn), lambda j, i: (i, j)),
        out_shape=jax.ShapeDtypeStruct((t, n), out_dtype),
        compiler_params=_params("parallel", "parallel"),
        name=name,
    )(a, w)


def _inproj_kernel(x_ref, g_ref, w_ref, gq_ref, qkv_ref, mlqk_ref, mlv_ref, mlo_ref, gif_ref, gift_ref, qn_ref,
                   *, widths):
    hb = _rms(x_ref[...], g_ref[...]).astype(BF16)
    sb3, ml2, mlv_w, mlo_w, xq_w, _ = widths
    step = INPROJ_COLS

    def proj(c0, c1):
        return _nt_dot(hb, w_ref[c0:c1, :])

    def emit(out_ref, base, width):
        for c in range(0, width, step):
            out_ref[:, c:c + step] = proj(base + c, base + c + step).astype(out_ref.dtype)

    emit(qkv_ref, 0, sb3)
    emit(mlqk_ref, sb3, ml2)
    emit(mlv_ref, sb3 + ml2, mlv_w)
    emit(mlo_ref, sb3 + ml2 + mlv_w, mlo_w)
    base = sb3 + ml2 + mlv_w + mlo_w
    dh = xq_w // X_HEADS
    for h in range(X_HEADS):
        qn_ref[:, h * dh:(h + 1) * dh] = _rms(proj(base + h * dh, base + (h + 1) * dh), gq_ref[...]).astype(BF16)
    gif_ref[...] = proj(base + xq_w, base + xq_w + gif_ref.shape[1])
    n_gates = gift_ref.shape[0]
    gift_ref[...] = _nt_dot(w_ref[base + xq_w:base + xq_w + LANES, :], hb)[:n_gates]


def _inproj(x2d, g, w_cat, gq, widths, tm):
    t, d = x2d.shape
    sb3, ml2, mlv_w, mlo_w, xq_w, gif_w = widths
    row = lambda n: pl.BlockSpec((tm, n), lambda i: (i, 0))
    const = lambda shape: pl.BlockSpec(shape, lambda i: (0, 0), pipeline_mode=pl.Buffered(1))
    return pl.pallas_call(
        functools.partial(_inproj_kernel, widths=widths),
        grid=(t // tm,),
        in_specs=[row(d), const((1, d)), const(w_cat.shape), const((1, xq_w // X_HEADS))],
        out_specs=[row(sb3), row(ml2), row(mlv_w), row(mlo_w), row(gif_w),
                   pl.BlockSpec((SUBLANES, tm), lambda i: (0, i)), row(xq_w)],
        out_shape=[jax.ShapeDtypeStruct((t, sb3), BF16), jax.ShapeDtypeStruct((t, ml2), F32),
                   jax.ShapeDtypeStruct((t, mlv_w), BF16), jax.ShapeDtypeStruct((t, mlo_w), F32),
                   jax.ShapeDtypeStruct((t, gif_w), F32), jax.ShapeDtypeStruct((SUBLANES, t), F32),
                   jax.ShapeDtypeStruct((t, xq_w), BF16)],
        compiler_params=_params("parallel"),
        name="in_proj",
    )(x2d, g.reshape(1, d), w_cat, gq.reshape(1, -1))


def _sb_kernel(q_ref, k_ref, v_ref, tri_ref, o_ref, acc_ref, carry_ref, z_ref, a_ref, *, scale):
    blk = SB_BLOCK
    tq = blk * SB_QBLOCKS
    tk = blk * SB_KBLOCKS
    qi = pl.program_id(2)

    def key_start(g):
        return pl.multiple_of(g * tk, tk)

    def mask_top(x):
        causal = (lax.broadcasted_iota(jnp.int32, (blk, blk), 1) < lax.broadcasted_iota(jnp.int32, (blk, blk), 0))
        top = jnp.where(causal, x[:blk], 0.0)
        return top if x.shape[0] == blk else jnp.concatenate([top, x[blk:]], axis=0)

    def logits(g, r_base=0):
        return _nt_dot(q_ref[0, r_base:, :], k_ref[0, pl.ds(key_start(g), tk), :])

    def weights(z_all, r_base, diag):
        a_blocks = [None] * SB_KBLOCKS
        for j in reversed(range(SB_KBLOCKS)):
            r0 = j * blk if diag else 0
            z = z_all[r0:, j * blk:(j + 1) * blk] * (scale * LOG2E)
            l = jnp.log(1.0 + jnp.exp2(-jnp.abs(z))) * LOG2E
            sp = jnp.maximum(z, 0.0) + l
            if diag:
                sp = mask_top(sp)
            hi = sp.astype(BF16)
            lo = (sp - hi.astype(F32)).astype(BF16)
            r = _dot(jnp.concatenate([hi, lo], axis=1), tri_ref[...])
            carry = carry_ref[r_base + r0:, :]
            a = jnp.exp2(z - r[:, :blk] - carry)
            if diag:
                a = mask_top(a)
            carry_ref[r_base + r0:, :] = carry + r[:, blk:]
            a = a.astype(BF16)
            a_blocks[j] = a if r0 == 0 else jnp.concatenate([jnp.zeros((r0, blk), BF16), a], axis=0)
        return jnp.concatenate(a_blocks, axis=1)

    def accumulate(a, g, r_base=0):
        acc_ref[r_base:, :] += _dot(a, v_ref[0, pl.ds(key_start(g), tk), :])

    carry_ref[...] = jnp.zeros_like(carry_ref)
    acc_ref[...] = jnp.zeros_like(acc_ref)
    n_diag = SB_QBLOCKS // SB_KBLOCKS
    assert n_diag * SB_KBLOCKS == SB_QBLOCKS and n_diag % 2 == 0
    n_full = qi * n_diag
    z_next = logits(n_full + n_diag - 1, (n_diag - 1) * tk)
    for d in reversed(range(n_diag)):
        z_cur = z_next
        if d > 0:
            z_next = logits(n_full + d - 1, (d - 1) * tk)
        else:
            z_ref[...] = logits(jnp.maximum(n_full - 1, 0))
        a = weights(z_cur, d * tk, True)
        if d > 0:
            accumulate(a, n_full + d, d * tk)
        else:
            a_ref[...] = a

    def body(i, c):
        ga = n_full - 1 - 2 * i
        gb = ga - 1
        z_b = logits(gb)
        accumulate(a_ref[...], ga + 1)
        a_a = weights(z_ref[...], 0, False)
        z_ref[...] = logits(jnp.maximum(ga - 2, 0))
        accumulate(a_a, ga)
        a_ref[...] = weights(z_b, 0, False)
        return c

    lax.fori_loop(0, n_full // 2, body, 0)
    accumulate(a_ref[...], 0)
    o_ref[0] = acc_ref[...].astype(o_ref.dtype)


def _sb_attention(qkv, n_heads):
    b, s, _ = qkv.shape
    blk = SB_BLOCK
    tq = blk * SB_QBLOCKS
    assert s % tq == 0
    row = lax.broadcasted_iota(jnp.int32, (2 * blk, 2 * blk), 0) % blk
    col = lax.broadcasted_iota(jnp.int32, (2 * blk, 2 * blk), 1)
    tri = jnp.where((col >= blk) | (row >= col), 1.0, 0.0).astype(BF16)
    return pl.pallas_call(
        functools.partial(_sb_kernel, scale=SB_HD ** -0.5),
        grid=(b, n_heads, s // tq),
        in_specs=[
            pl.BlockSpec((1, tq, SB_HD), lambda bi, h, i: (bi, i, h)),
            pl.BlockSpec((1, s, SB_HD), lambda bi, h, i: (bi, 0, n_heads + h)),
            pl.BlockSpec((1, s, SB_HD), lambda bi, h, i: (bi, 0, 2 * n_heads + h)),
            pl.BlockSpec((2 * blk, 2 * blk), lambda bi, h, i: (0, 0)),
        ],
        out_specs=pl.BlockSpec((1, tq, SB_HD), lambda bi, h, i: (bi, i, h)),
        out_shape=jax.ShapeDtypeStruct((b, s, n_heads * SB_HD), BF16),
        scratch_shapes=[pltpu.VMEM((tq, SB_HD), F32), pltpu.VMEM((tq, blk), F32),
                        pltpu.VMEM((tq, blk * SB_KBLOCKS), F32), pltpu.VMEM((tq, blk * SB_KBLOCKS), BF16)],
        compiler_params=_params("parallel", "parallel", "arbitrary"),
        name="sb_attention",
    )(qkv, qkv, qkv, tri)


def _silu(u):
    half = 0.5 * u
    return half + half * jnp.tanh(half)


def _conv_silu(raw_refs, w_ref, b_ref, lane0, halo_ref, out_ref, out_scale):
    P = CONV_PITCH
    rows = SUBLANES * P
    first = lax.broadcasted_iota(jnp.int32, (SUBLANES, LANES), 0) == 0
    for half, raw_ref in enumerate(raw_refs):
        lanes = slice(lane0 + half * LANES, lane0 + (half + 1) * LANES)
        w = [jnp.broadcast_to(w_ref[j:j + 1, lanes], (SUBLANES, LANES)) for j in range(CONV_W)]
        bias = jnp.broadcast_to(b_ref[:, lanes], (SUBLANES, LANES))
        tail = [halo_ref[half, m] for m in range(CONV_W - 1)]
        for blk in range(raw_ref.shape[1] // rows):
            base = blk * rows
            x = [raw_ref[0, pl.ds(base + i, SUBLANES, stride=P), :] for i in range(P)]
            moved = [pltpu.roll(x[P - (CONV_W - 1) + m], 1, axis=0) for m in range(CONV_W - 1)]
            virt = [jnp.where(first, tail[m], moved[m]) for m in range(CONV_W - 1)]
            ext = virt + x
            for i in range(P):
                u = bias
                for j in range(CONV_W):
                    u = u + w[j] * ext[i + j]
                y = _silu(u)
                out_ref[half, pl.ds(base + i, SUBLANES, stride=P), :] = y if out_scale == 1.0 else y * out_scale
            tail = moved
        for m in range(CONV_W - 1):
            halo_ref[half, m] = tail[m]


def _mlstm_kernel(*refs, k_scale, dh):
    nh = ML_HEADS
    L = ML_CHUNK
    bif_ref = refs[0]
    q_slabs, k_slabs = refs[1:1 + 2 * nh], refs[1 + 2 * nh:1 + 4 * nh]
    v_ref, o_ref, gcol_ref, grow_ref, cw_ref, cb_ref, ng_ref, y_ref = refs[1 + 4 * nh:9 + 4 * nh]
    qhalo, khalo, qs_ref, ks_ref, ct_ref, n_ref, m_ref = refs[9 + 4 * nh:]
    c = pl.program_id(1)
    heads = range(nh)

    @pl.when(c == 0)
    def _():
        qhalo[...] = jnp.zeros_like(qhalo)
        khalo[...] = jnp.zeros_like(khalo)
        ct_ref[...] = jnp.zeros_like(ct_ref)
        n_ref[...] = jnp.zeros_like(n_ref)
        m_ref[...] = jnp.zeros_like(m_ref)

    col = lambda h: slice(h * dh, (h + 1) * dh)
    for h in heads:
        _conv_silu(q_slabs[2 * h:2 * h + 2], cw_ref, cb_ref, h * dh, qhalo.at[h], qs_ref.at[h], 1.0)
        _conv_silu(k_slabs[2 * h:2 * h + 2], cw_ref, cb_ref, (nh + h) * dh, khalo.at[h], ks_ref.at[h], k_scale)

    t_idx = lax.broadcasted_iota(jnp.int32, (L, L), 0)
    s_idx = lax.broadcasted_iota(jnp.int32, (L, L), 1)
    tril = s_idx <= t_idx
    triu = t_idx <= s_idx

    gate = []
    for h in heads:
        b_i = bif_ref[h]
        b_f = bif_ref[nh + h]
        i_col = gcol_ref[0, :, h:h + 1] + b_i
        lf_col = jax.nn.log_sigmoid(gcol_ref[0, :, nh + h:nh + h + 1] + b_f)
        i_row = grow_ref[h:h + 1, :] + b_i
        lf_row = jax.nn.log_sigmoid(grow_ref[nh + h:nh + h + 1, :] + b_f)
        bcum_col = jnp.sum(jnp.where(tril, lf_row, 0.0), axis=1, keepdims=True)
        bcum_row = jnp.sum(jnp.where(triu, lf_col, 0.0), axis=0, keepdims=True)
        m_prev = m_ref[h]
        d = jnp.where(tril, bcum_col - bcum_row + i_row, -jnp.inf)
        m_inter = bcum_col + m_prev
        m_t = jnp.maximum(m_inter, jnp.max(d, axis=1, keepdims=True))
        b_end = jnp.sum(lf_row, axis=1, keepdims=True)
        g_col = b_end - bcum_col + i_col
        m_new = jnp.maximum(b_end + m_prev, jnp.max(g_col, axis=0, keepdims=True))
        gate.append(dict(w=jnp.exp(d - m_t), s_inter=jnp.exp(m_inter - m_t), floor=jnp.exp(-m_t),
                         decay=jnp.exp(b_end + m_prev - m_new), wk=jnp.exp(g_col - m_new), m_new=m_new))

    q = [jnp.concatenate([qs_ref[h, 0], qs_ref[h, 1]], axis=1) for h in heads]
    k = [jnp.concatenate([ks_ref[h, 0], ks_ref[h, 1]], axis=1) for h in heads]
    qb = [x.astype(BF16) for x in q]
    kb = [x.astype(BF16) for x in k]
    qk = [_nt_dot(qb[h], kb[h]) for h in heads]
    inter = [_dot(qb[h], ct_ref[h].astype(BF16)) for h in heads]

    for h in heads:
        g = gate[h]
        v = v_ref[0, :, col(h)]
        sc = qk[h] * g["w"]
        num = _dot(sc.astype(BF16), v) + g["s_inter"] * inter[h]
        den = jnp.sum(sc, axis=1, keepdims=True) + g["s_inter"] * jnp.sum(q[h] * n_ref[h], axis=1, keepdims=True)
        hh = num / jnp.maximum(jnp.abs(den), g["floor"])
        hn = _rms(hh, ng_ref[:, col(h)])
        y_ref[0, :, col(h)] = (hn * (0.5 + 0.5 * jnp.tanh(0.5 * o_ref[0, :, col(h)]))).astype(y_ref.dtype)

    for h in heads:
        g = gate[h]
        vw = (v_ref[0, :, col(h)].astype(F32) * g["wk"]).astype(BF16)
        ct_ref[h] = g["decay"] * ct_ref[h] + _tn_dot(kb[h], vw)
        n_ref[h] = g["decay"] * n_ref[h] + jnp.sum(k[h] * g["wk"], axis=0, keepdims=True)
        m_ref[h] = g["m_new"]


def _mlstm(mlqk, mlv, mlo, gif, gif_t, b_if, conv_w, conv_b, norm_g):
    b, s, w2 = mlqk.shape
    wd = w2 // 2
    nh = ML_HEADS
    dh = wd // nh
    assert dh == 2 * LANES
    L = ML_CHUNK
    nc = s // L
    slab = lambda j: pl.BlockSpec((1, L, LANES), lambda bi, c: (bi, c, j))
    full = lambda n: pl.BlockSpec((1, L, n), lambda bi, c: (bi, c, 0))
    const = lambda shape: pl.BlockSpec(shape, lambda bi, c: (0, 0))
    n_slabs = w2 // LANES
    halo = pltpu.VMEM((nh, 2, CONV_W - 1, SUBLANES, LANES), F32)
    slabs = pltpu.VMEM((nh, 2, L, LANES), F32)
    return pl.pallas_call(
        functools.partial(_mlstm_kernel, k_scale=dh ** -0.5, dh=dh),
        grid=(b, nc),
        in_specs=[pl.BlockSpec(memory_space=pltpu.SMEM)] + [slab(j) for j in range(n_slabs)] + [
            full(wd), full(wd), full(LANES),
            pl.BlockSpec((SUBLANES, L), lambda bi, c: (0, bi * nc + c)),
            const((CONV_W, w2)), const((1, w2)), const((1, wd)),
        ],
        out_specs=full(wd),
        out_shape=jax.ShapeDtypeStruct((b, s, wd), BF16),
        scratch_shapes=[halo, halo, slabs, slabs, pltpu.VMEM((nh, dh, dh), F32), pltpu.VMEM((nh, 1, dh), F32),
                        pltpu.VMEM((nh, 1, 1), F32)],
        compiler_params=_params("parallel", "arbitrary"),
        name="mlstm",
    )(b_if, *([mlqk] * n_slabs), mlv, mlo, gif, gif_t, conv_w, conv_b.reshape(1, w2), norm_g.reshape(1, wd))


def _kv_prep_kernel(kv_ref, gk_ref, kn_ref, mv_ref):
    xw = kn_ref.shape[1]
    dh = xw // X_HEADS
    for h in range(X_HEADS):
        kh = kv_ref[:, h * dh:(h + 1) * dh]
        kn_ref[:, h * dh:(h + 1) * dh] = _rms(kh, gk_ref[...]).astype(kn_ref.dtype)
    mv_ref[...] = kv_ref[:, xw:].astype(mv_ref.dtype)


def _kv_prep(kv, gk, tm):
    t, w2 = kv.shape
    xw = w2 // 2
    return pl.pallas_call(
        _kv_prep_kernel,
        grid=(t // tm,),
        in_specs=[pl.BlockSpec((tm, w2), lambda i: (i, 0)), pl.BlockSpec((1, xw // X_HEADS), lambda i: (0, 0))],
        out_specs=[pl.BlockSpec((tm, xw), lambda i: (i, 0)), pl.BlockSpec((tm, xw), lambda i: (i, 0))],
        out_shape=[jax.ShapeDtypeStruct((t, xw), BF16), jax.ShapeDtypeStruct((t, xw), BF16)],
        compiler_params=_params("parallel"),
        name="kv_prep",
    )(kv, gk.reshape(1, -1))


def _xattn_kernel(q_ref, kn_ref, mv_ref, o_ref):
    xw = q_ref.shape[2]
    dh = xw // X_HEADS
    for h in range(X_HEADS):
        sl = slice(h * dh, (h + 1) * dh)
        logits = _nt_dot(q_ref[0, :, sl], kn_ref[0, :, sl]) * (dh ** -0.5)
        e = jnp.exp(logits - jnp.max(logits, axis=-1, keepdims=True))
        p = e / jnp.sum(e, axis=-1, keepdims=True)
        o_ref[0, :, sl] = _dot(p.astype(BF16), mv_ref[0, :, sl]).astype(o_ref.dtype)


def _cross_attention(qn, kn, mv, tm):
    b, s, w = qn.shape
    m = kn.shape[1]
    return pl.pallas_call(
        _xattn_kernel,
        grid=(b, s // tm),
        in_specs=[
            pl.BlockSpec((1, tm, w), lambda bi, i: (bi, i, 0)),
            pl.BlockSpec((1, m, w), lambda bi, i: (bi, 0, 0)),
            pl.BlockSpec((1, m, w), lambda bi, i: (bi, 0, 0)),
        ],
        out_specs=pl.BlockSpec((1, tm, w), lambda bi, i: (bi, i, 0)),
        out_shape=jax.ShapeDtypeStruct((b, s, w), BF16),
        compiler_params=_params("parallel", "parallel"),
        name="cross_attention",
    )(qn, kn, mv)


def _merge_kernel(x_ref, g_ref, ysb_ref, yml_ref, yx_ref, wg_ref, bg_ref, wsb_ref, wml_ref, wx_ref, wo_ref, o_ref):
    d = x_ref.shape[1]
    h = _rms(x_ref[...], g_ref[...]).astype(BF16)
    mixed = None
    for i, (y_ref, w_ref) in enumerate(((ysb_ref, wsb_ref), (yml_ref, wml_ref), (yx_ref, wx_ref))):
        gate = jax.nn.sigmoid(_nt_dot(h, wg_ref[i * d:(i + 1) * d, :]) + bg_ref[:, i * d:(i + 1) * d])
        term = gate * _dot(y_ref[...], w_ref[...])
        mixed = term if mixed is None else mixed + term
    o_ref[...] = x_ref[...] + _dot(mixed.astype(BF16), wo_ref[...])


def _merge(x2d, g, ysb, yml, yx, wg, bg, wsb, wml, wx, wo, tm):
    t, d = x2d.shape
    row = lambda: pl.BlockSpec((tm, d), lambda i: (i, 0))
    const = lambda shape: pl.BlockSpec(shape, lambda i: (0, 0), pipeline_mode=pl.Buffered(1))
    return pl.pallas_call(
        _merge_kernel,
        grid=(t // tm,),
        in_specs=[row(), const((1, d)), row(), row(), row(), const((N_BRANCH * d, d)), const((1, N_BRANCH * d)),
                  const((d, d)), const((d, d)), const((d, d)), const((d, d))],
        out_specs=row(),
        out_shape=jax.ShapeDtypeStruct((t, d), F32),
        compiler_params=_params("parallel"),
        name="merge_out",
    )(x2d, g.reshape(1, d), ysb, yml, yx, wg, bg.reshape(1, -1), wsb, wml, wx, wo)


def _mlp_kernel(x_ref, g_ref, w1_ref, w2_ref, o_ref, *, ff_chunk):
    x = x_ref[...]
    hb = _rms(x, g_ref[...]).astype(BF16)
    acc = x
    for c in range(w1_ref.shape[1] // ff_chunk):
        u = _dot(hb, w1_ref[:, c * ff_chunk:(c + 1) * ff_chunk])
        a = jnp.square(jnp.maximum(u, 0.0)).astype(BF16)
        acc = acc + _dot(a, w2_ref[c * ff_chunk:(c + 1) * ff_chunk, :])
    o_ref[...] = acc


def _mlp(x2d, g, w1, w2, tm, ff_chunk):
    t, d = x2d.shape
    ff = w1.shape[1]
    const = lambda shape: pl.BlockSpec(shape, lambda i: (0, 0), pipeline_mode=pl.Buffered(1))
    return pl.pallas_call(
        functools.partial(_mlp_kernel, ff_chunk=ff_chunk),
        grid=(t // tm,),
        in_specs=[pl.BlockSpec((tm, d), lambda i: (i, 0)), const((1, d)), const((d, ff)), const((ff, d))],
        out_specs=pl.BlockSpec((tm, d), lambda i: (i, 0)),
        out_shape=jax.ShapeDtypeStruct((t, d), F32),
        compiler_params=_params("parallel"),
        name="mlp",
    )(x2d, g.reshape(1, d), w1, w2)


def _layer(x, mem, g_mix, w_in, b_if, b_gate, conv_w, conv_b, ml_norm_g, g_mem, w_mem_kv, q_norm_g, k_norm_g,
           w_sb_proj, w_ml_proj, w_x_proj, w_out, g_mlp, w_ff1, w_ff2):
    b, s, d = x.shape
    m = mem.shape[1]
    t = b * s
    sb_w = w_sb_proj.shape[0]
    ml_w = w_ml_proj.shape[0]
    x_w = w_x_proj.shape[0]
    sb_heads = sb_w // SB_HD
    tm = min(512, t)

    sizes = (sb_w, sb_w, sb_w, ml_w, ml_w, ml_w, ml_w, ML_HEADS, ML_HEADS, x_w, N_BRANCH * d)
    offs = [0]
    for sz in sizes:
        offs.append(offs[-1] + sz)
    w_in_t = w_in.T
    col = lambda a, bnd: w_in_t[offs[a]:offs[bnd]].astype(BF16)

    x2d = x.reshape(t, d)
    assert 2 * ML_HEADS == SUBLANES
    w_gif = jnp.pad(col(7, 9), ((0, LANES - 2 * ML_HEADS), (0, 0)))
    w_cat = jnp.concatenate([col(0, 7), col(9, 10), w_gif], axis=0)
    qkv, mlqk, mlv, mlo, gif, gif_t, qn = _inproj(x2d, g_mix, w_cat, q_norm_g,
                                                  (3 * sb_w, 2 * ml_w, ml_w, ml_w, x_w, LANES), tm)

    y_sb = _sb_attention(qkv.reshape(b, s, 3 * sb_w), sb_heads).reshape(t, sb_w)

    y_ml = _mlstm(mlqk.reshape(b, s, 2 * ml_w), mlv.reshape(b, s, ml_w), mlo.reshape(b, s, ml_w),
                  gif.reshape(b, s, LANES), gif_t, b_if, conv_w, conv_b, ml_norm_g).reshape(t, ml_w)

    tmm = min(512, b * m)
    hm = _norm_cast(mem.reshape(b * m, d), g_mem, tmm)
    kv = _matmul(hm, w_mem_kv.astype(BF16), F32, tmm, min(1024, 2 * x_w), "proj_mem_kv")
    kn, mv = _kv_prep(kv, k_norm_g, tmm)
    y_x = _cross_attention(qn.reshape(b, s, x_w), kn.reshape(b, m, x_w), mv.reshape(b, m, x_w),
                           min(512, s)).reshape(t, x_w)

    x1 = _merge(x2d, g_mix, y_sb, y_ml, y_x, col(10, 11), b_gate, w_sb_proj.astype(BF16),
                w_ml_proj.astype(BF16), w_x_proj.astype(BF16), w_out.astype(BF16), tm)
    x2 = _mlp(x1, g_mlp, w_ff1.astype(BF16), w_ff2.astype(BF16), tm, min(1024, w_ff1.shape[1]))
    return x2.reshape(b, s, d)


def kernel(x, mem, g_mix, w_in, b_if, b_gate, conv_w, conv_b, ml_norm_g, g_mem, w_mem_kv, q_norm_g, k_norm_g,
           w_sb_proj, w_ml_proj, w_x_proj, w_out, g_mlp, w_ff1, w_ff2):
    for l in range(g_mix.shape[0]):
        x = _layer(x, mem, g_mix[l], w_in[l], b_if[l], b_gate[l], conv_w[l], conv_b[l], ml_norm_g[l], g_mem[l],
                   w_mem_kv[l], q_norm_g[l], k_norm_g[l], w_sb_proj[l], w_ml_proj[l], w_x_proj[l], w_out[l],
                   g_mlp[l], w_ff1[l], w_ff2[l])
    return x
```

```python
import functools

import jax
import jax.numpy as jnp
from jax import lax
from jax.experimental import pallas as pl
from jax.experimental.pallas import tpu as pltpu

EPS = 1e-6
LOG2E = 1.4426950408889634
SB_HD = 128
ML_HEADS = 4
X_HEADS = 4
CONV_W = 4
N_BRANCH = 3
SB_BLOCK = 256
SB_QBLOCKS = 8
SB_KBLOCKS = 2
INPROJ_COLS = 1024
ML_CHUNK = 256
CONV_PITCH = 4
SUBLANES = 8
LANES = 128
VMEM_LIMIT = 56 * 1024 * 1024

F32 = jnp.float32
BF16 = jnp.bfloat16


def _params(*sem):
    return pltpu.CompilerParams(dimension_semantics=sem, vmem_limit_bytes=VMEM_LIMIT)


def _nt_dot(a, b):
    return lax.dot_general(a, b, (((1,), (1,)), ((), ())), preferred_element_type=F32)


def _tn_dot(a, b):
    return lax.dot_general(a, b, (((0,), (0,)), ((), ())), preferred_element_type=F32)


def _dot(a, b):
    return jnp.dot(a, b, preferred_element_type=F32)


def _rms(x, g):
    return x * lax.rsqrt(jnp.mean(x * x, axis=-1, keepdims=True) + EPS) * g


def _inproj_kernel(x_ref, g_ref, w_ref, wxq_ref, wgif_ref, gq_ref, qkv_ref, mlqk_ref, mlv_ref, mlo_ref, gif_ref,
                   gift_ref, qn_ref, *, widths, q_scale):
    hb = _rms(x_ref[...], g_ref[...]).astype(BF16)
    sb3, ml2, mlv_w, mlo_w, xq_w, _ = widths
    step = INPROJ_COLS

    def proj(c0, c1, ref=w_ref):
        return _nt_dot(hb, ref[c0:c1, :])

    def emit(out_ref, base, width, scale_to=0):
        for c in range(0, width, step):
            y = proj(base + c, base + c + step)
            out_ref[:, c:c + step] = (y * q_scale if c < scale_to else y).astype(out_ref.dtype)

    emit(qkv_ref, 0, sb3, sb3 // 3)
    emit(mlqk_ref, sb3, ml2)
    emit(mlv_ref, sb3 + ml2, mlv_w)
    emit(mlo_ref, sb3 + ml2 + mlv_w, mlo_w)
    dh = xq_w // X_HEADS
    for h in range(X_HEADS):
        qn_ref[:, h * dh:(h + 1) * dh] = _rms(proj(h * dh, (h + 1) * dh, wxq_ref), gq_ref[...]).astype(BF16)
    gif_ref[...] = _nt_dot(hb, wgif_ref[...])
    gift_ref[...] = _nt_dot(wgif_ref[...], hb)[:gift_ref.shape[0]]


def _inproj(x2d, g, w_t, w_xq, w_gif, gq, widths, tm):
    t, d = x2d.shape
    sb3, ml2, mlv_w, mlo_w, xq_w, gif_w = widths
    assert sb3 % (3 * INPROJ_COLS) == 0
    row = lambda n: pl.BlockSpec((tm, n), lambda i: (i, 0))
    const = lambda shape: pl.BlockSpec(shape, lambda i: (0, 0), pipeline_mode=pl.Buffered(1))
    return pl.pallas_call(
        functools.partial(_inproj_kernel, widths=widths, q_scale=SB_HD ** -0.5 * LOG2E),
        grid=(t // tm,),
        in_specs=[row(d), const((1, d)), const((sb3 + ml2 + mlv_w + mlo_w, d)), const(w_xq.shape), const(w_gif.shape),
                  const((1, xq_w // X_HEADS))],
        out_specs=[row(sb3), row(ml2), row(mlv_w), row(mlo_w), row(gif_w),
                   pl.BlockSpec((SUBLANES, tm), lambda i: (0, i)), row(xq_w)],
        out_shape=[jax.ShapeDtypeStruct((t, sb3), BF16), jax.ShapeDtypeStruct((t, ml2), F32),
                   jax.ShapeDtypeStruct((t, mlv_w), BF16), jax.ShapeDtypeStruct((t, mlo_w), F32),
                   jax.ShapeDtypeStruct((t, gif_w), F32), jax.ShapeDtypeStruct((SUBLANES, t), F32),
                   jax.ShapeDtypeStruct((t, xq_w), BF16)],
        compiler_params=_params("parallel"),
        name="in_proj",
    )(x2d, g.reshape(1, d), w_t, w_xq, w_gif, gq.reshape(1, -1))


def _sb_kernel(q_ref, k_ref, v_ref, tri_ref, o_ref, acc_ref, carry_ref, z_ref, a_ref):
    blk = SB_BLOCK
    tq = blk * SB_QBLOCKS
    tk = blk * SB_KBLOCKS
    qi = pl.program_id(2)

    def key_start(g):
        return pl.multiple_of(g * tk, tk)

    def mask_top(x):
        causal = (lax.broadcasted_iota(jnp.int32, (blk, blk), 1) < lax.broadcasted_iota(jnp.int32, (blk, blk), 0))
        top = jnp.where(causal, x[:blk], 0.0)
        return top if x.shape[0] == blk else jnp.concatenate([top, x[blk:]], axis=0)

    def logits(g, r_base=0):
        return _nt_dot(q_ref[0, r_base:, :], k_ref[0, pl.ds(key_start(g), tk), :])

    def weights(z_all, r_base, diag):
        a_blocks = [None] * SB_KBLOCKS
        for j in reversed(range(SB_KBLOCKS)):
            r0 = j * blk if diag else 0
            z = z_all[r0:, j * blk:(j + 1) * blk]
            l = jnp.log(1.0 + jnp.exp2(-jnp.abs(z))) * LOG2E
            sp = jnp.maximum(z, 0.0) + l
            log_beta = z - sp
            if diag:
                sp = mask_top(sp)
            later = _dot(sp.astype(BF16), tri_ref[...])
            carry = carry_ref[r_base + r0:, :]
            a = jnp.exp2(log_beta - later - jnp.concatenate([carry] * (blk // LANES), axis=1))
            if diag:
                a = mask_top(a)
            carry_ref[r_base + r0:, :] = carry + (later[:, 0:1] + sp[:, 0:1])
            a = a.astype(BF16)
            a_blocks[j] = a if r0 == 0 else jnp.concatenate([jnp.zeros((r0, blk), BF16), a], axis=0)
        return jnp.concatenate(a_blocks, axis=1)

    def accumulate(a, g, r_base=0):
        acc_ref[r_base:, :] += _dot(a, v_ref[0, pl.ds(key_start(g), tk), :])

    carry_ref[...] = jnp.zeros_like(carry_ref)
    acc_ref[...] = jnp.zeros_like(acc_ref)
    n_diag = SB_QBLOCKS // SB_KBLOCKS
    assert n_diag * SB_KBLOCKS == SB_QBLOCKS and n_diag % 2 == 0
    n_full = qi * n_diag
    z_next = logits(n_full + n_diag - 1, (n_diag - 1) * tk)
    for d in reversed(range(n_diag)):
        z_cur = z_next
        if d > 0:
            z_next = logits(n_full + d - 1, (d - 1) * tk)
        else:
            z_ref[...] = logits(jnp.maximum(n_full - 1, 0))
        a = weights(z_cur, d * tk, True)
        if d > 0:
            accumulate(a, n_full + d, d * tk)
        else:
            a_ref[...] = a

    def body(i, c):
        ga = n_full - 1 - 2 * i
        gb = ga - 1
        z_b = logits(gb)
        accumulate(a_ref[...], ga + 1)
        a_a = weights(z_ref[...], 0, False)
        z_ref[...] = logits(jnp.maximum(ga - 2, 0))
        accumulate(a_a, ga)
        a_ref[...] = weights(z_b, 0, False)
        return c

    lax.fori_loop(0, n_full // 2, body, 0)
    accumulate(a_ref[...], 0)
    o_ref[0] = acc_ref[...].astype(o_ref.dtype)


def _sb_attention(qkv, n_heads):
    b, s, _ = qkv.shape
    blk = SB_BLOCK
    tq = blk * SB_QBLOCKS
    assert s % tq == 0
    tri = jnp.where(lax.broadcasted_iota(jnp.int32, (blk, blk), 0) > lax.broadcasted_iota(jnp.int32, (blk, blk), 1),
                    1.0, 0.0).astype(BF16)
    return pl.pallas_call(
        _sb_kernel,
        grid=(b, n_heads, s // tq),
        in_specs=[
            pl.BlockSpec((1, tq, SB_HD), lambda bi, h, i: (bi, i, h)),
            pl.BlockSpec((1, s, SB_HD), lambda bi, h, i: (bi, 0, n_heads + h)),
            pl.BlockSpec((1, s, SB_HD), lambda bi, h, i: (bi, 0, 2 * n_heads + h)),
            pl.BlockSpec((blk, blk), lambda bi, h, i: (0, 0)),
        ],
        out_specs=pl.BlockSpec((1, tq, SB_HD), lambda bi, h, i: (bi, i, h)),
        out_shape=jax.ShapeDtypeStruct((b, s, n_heads * SB_HD), BF16),
        scratch_shapes=[pltpu.VMEM((tq, SB_HD), F32), pltpu.VMEM((tq, LANES), F32),
                        pltpu.VMEM((tq, blk * SB_KBLOCKS), F32), pltpu.VMEM((tq, blk * SB_KBLOCKS), BF16)],
        compiler_params=_params("parallel", "parallel", "arbitrary"),
        name="sb_attention",
    )(qkv, qkv, qkv, tri)


def _silu(u):
    half = 0.5 * u
    return half + half * jnp.tanh(half)


def _conv_silu(raw_refs, w_ref, b_ref, lane0, halo_ref, out_ref, out_scale):
    P = CONV_PITCH
    rows = SUBLANES * P
    first = lax.broadcasted_iota(jnp.int32, (SUBLANES, LANES), 0) == 0
    for half, raw_ref in enumerate(raw_refs):
        lanes = slice(lane0 + half * LANES, lane0 + (half + 1) * LANES)
        w = [jnp.broadcast_to(w_ref[j:j + 1, lanes], (SUBLANES, LANES)) for j in range(CONV_W)]
        bias = jnp.broadcast_to(b_ref[:, lanes], (SUBLANES, LANES))
        tail = [halo_ref[half, m] for m in range(CONV_W - 1)]
        for blk in range(raw_ref.shape[1] // rows):
            base = blk * rows
            x = [raw_ref[0, pl.ds(base + i, SUBLANES, stride=P), :] for i in range(P)]
            moved = [pltpu.roll(x[P - (CONV_W - 1) + m], 1, axis=0) for m in range(CONV_W - 1)]
            virt = [jnp.where(first, tail[m], moved[m]) for m in range(CONV_W - 1)]
            ext = virt + x
            for i in range(P):
                u = bias
                for j in range(CONV_W):
                    u = u + w[j] * ext[i + j]
                y = _silu(u)
                out_ref[half, pl.ds(base + i, SUBLANES, stride=P), :] = y if out_scale == 1.0 else y * out_scale
            tail = moved
        for m in range(CONV_W - 1):
            halo_ref[half, m] = tail[m]


def _mlstm_kernel(*refs, k_scale, dh):
    nh = ML_HEADS
    L = ML_CHUNK
    bif_ref = refs[0]
    q_slabs, k_slabs = refs[1:1 + 2 * nh], refs[1 + 2 * nh:1 + 4 * nh]
    v_ref, o_ref, gcol_ref, grow_ref, cw_ref, cb_ref, ng_ref, y_ref = refs[1 + 4 * nh:9 + 4 * nh]
    qhalo, khalo, qs_ref, ks_ref, ct_ref, n_ref, m_ref = refs[9 + 4 * nh:]
    c = pl.program_id(1)
    heads = range(nh)

    @pl.when(c == 0)
    def _():
        qhalo[...] = jnp.zeros_like(qhalo)
        khalo[...] = jnp.zeros_like(khalo)
        ct_ref[...] = jnp.zeros_like(ct_ref)
        n_ref[...] = jnp.zeros_like(n_ref)
        m_ref[...] = jnp.zeros_like(m_ref)

    col = lambda h: slice(h * dh, (h + 1) * dh)
    for h in heads:
        _conv_silu(q_slabs[2 * h:2 * h + 2], cw_ref, cb_ref, h * dh, qhalo.at[h], qs_ref.at[h], 1.0)
        _conv_silu(k_slabs[2 * h:2 * h + 2], cw_ref, cb_ref, (nh + h) * dh, khalo.at[h], ks_ref.at[h], k_scale)

    t_idx = lax.broadcasted_iota(jnp.int32, (L, L), 0)
    s_idx = lax.broadcasted_iota(jnp.int32, (L, L), 1)
    tril = s_idx <= t_idx
    triu = t_idx <= s_idx

    gate = []
    for h in heads:
        b_i = bif_ref[h]
        b_f = bif_ref[nh + h]
        i_col = gcol_ref[0, :, h:h + 1] + b_i
        lf_col = jax.nn.log_sigmoid(gcol_ref[0, :, nh + h:nh + h + 1] + b_f)
        i_row = grow_ref[h:h + 1, :] + b_i
        lf_row = jax.nn.log_sigmoid(grow_ref[nh + h:nh + h + 1, :] + b_f)
        bcum_col = jnp.sum(jnp.where(tril, lf_row, 0.0), axis=1, keepdims=True)
        bcum_row = jnp.sum(jnp.where(triu, lf_col, 0.0), axis=0, keepdims=True)
        m_prev = m_ref[h]
        d = jnp.where(tril, bcum_col - bcum_row + i_row, -jnp.inf)
        m_inter = bcum_col + m_prev
        m_t = jnp.maximum(m_inter, jnp.max(d, axis=1, keepdims=True))
        b_end = jnp.sum(lf_row, axis=1, keepdims=True)
        g_col = b_end - bcum_col + i_col
        m_new = jnp.maximum(b_end + m_prev, jnp.max(g_col, axis=0, keepdims=True))
        gate.append(dict(w=jnp.exp(d - m_t), s_inter=jnp.exp(m_inter - m_t), floor=jnp.exp(-m_t),
                         decay=jnp.exp(b_end + m_prev - m_new), wk=jnp.exp(g_col - m_new), m_new=m_new))

    q = [jnp.concatenate([qs_ref[h, 0], qs_ref[h, 1]], axis=1) for h in heads]
    k = [jnp.concatenate([ks_ref[h, 0], ks_ref[h, 1]], axis=1) for h in heads]
    qb = [x.astype(BF16) for x in q]
    kb = [x.astype(BF16) for x in k]
    qk = [_nt_dot(qb[h], kb[h]) for h in heads]
    inter = [_dot(qb[h], ct_ref[h].astype(BF16)) for h in heads]

    for h in heads:
        g = gate[h]
        v = v_ref[0, :, col(h)]
        sc = qk[h] * g["w"]
        num = _dot(sc.astype(BF16), v) + g["s_inter"] * inter[h]
        den = jnp.sum(sc, axis=1, keepdims=True) + g["s_inter"] * jnp.sum(q[h] * n_ref[h], axis=1, keepdims=True)
        hh = num / jnp.maximum(jnp.abs(den), g["floor"])
        hn = _rms(hh, ng_ref[:, col(h)])
        y_ref[0, :, col(h)] = (hn * (0.5 + 0.5 * jnp.tanh(0.5 * o_ref[0, :, col(h)]))).astype(y_ref.dtype)

    for h in heads:
        g = gate[h]
        vw = (v_ref[0, :, col(h)].astype(F32) * g["wk"]).astype(BF16)
        ct_ref[h] = g["decay"] * ct_ref[h] + _tn_dot(kb[h], vw)
        n_ref[h] = g["decay"] * n_ref[h] + jnp.sum(k[h] * g["wk"], axis=0, keepdims=True)
        m_ref[h] = g["m_new"]


def _mlstm(mlqk, mlv, mlo, gif, gif_t, b_if, conv_w, conv_b, norm_g):
    b, s, w2 = mlqk.shape
    wd = w2 // 2
    nh = ML_HEADS
    dh = wd // nh
    assert dh == 2 * LANES
    L = ML_CHUNK
    nc = s // L
    slab = lambda j: pl.BlockSpec((1, L, LANES), lambda bi, c: (bi, c, j))
    full = lambda n: pl.BlockSpec((1, L, n), lambda bi, c: (bi, c, 0))
    const = lambda shape: pl.BlockSpec(shape, lambda bi, c: (0, 0))
    n_slabs = w2 // LANES
    halo = pltpu.VMEM((nh, 2, CONV_W - 1, SUBLANES, LANES), F32)
    slabs = pltpu.VMEM((nh, 2, L, LANES), F32)
    return pl.pallas_call(
        functools.partial(_mlstm_kernel, k_scale=dh ** -0.5, dh=dh),
        grid=(b, nc),
        in_specs=[pl.BlockSpec(memory_space=pltpu.SMEM)] + [slab(j) for j in range(n_slabs)] + [
            full(wd), full(wd), full(LANES),
            pl.BlockSpec((SUBLANES, L), lambda bi, c: (0, bi * nc + c)),
            const((CONV_W, w2)), const((1, w2)), const((1, wd)),
        ],
        out_specs=full(wd),
        out_shape=jax.ShapeDtypeStruct((b, s, wd), BF16),
        scratch_shapes=[halo, halo, slabs, slabs, pltpu.VMEM((nh, dh, dh), F32), pltpu.VMEM((nh, 1, dh), F32),
                        pltpu.VMEM((nh, 1, 1), F32)],
        compiler_params=_params("parallel", "arbitrary"),
        name="mlstm",
    )(b_if, *([mlqk] * n_slabs), mlv, mlo, gif, gif_t, conv_w, conv_b.reshape(1, w2), norm_g.reshape(1, wd))


def _mem_kv_kernel(m_ref, g_ref, w_ref, gk_ref, kn_ref, mv_ref):
    xw = kn_ref.shape[1]
    dh = xw // X_HEADS
    kv = _dot(_rms(m_ref[...], g_ref[...]).astype(BF16), w_ref[...])
    for h in range(X_HEADS):
        kn_ref[:, h * dh:(h + 1) * dh] = _rms(kv[:, h * dh:(h + 1) * dh], gk_ref[...]).astype(kn_ref.dtype)
    mv_ref[...] = kv[:, xw:].astype(mv_ref.dtype)


def _mem_kv(mem2d, g, w, gk, tm):
    t, d = mem2d.shape
    xw = w.shape[1] // 2
    const = lambda shape: pl.BlockSpec(shape, lambda i: (0, 0))
    return pl.pallas_call(
        _mem_kv_kernel,
        grid=(t // tm,),
        in_specs=[pl.BlockSpec((tm, d), lambda i: (i, 0)), const((1, d)), const(w.shape), const((1, xw // X_HEADS))],
        out_specs=[pl.BlockSpec((tm, xw), lambda i: (i, 0)), pl.BlockSpec((tm, xw), lambda i: (i, 0))],
        out_shape=[jax.ShapeDtypeStruct((t, xw), BF16), jax.ShapeDtypeStruct((t, xw), BF16)],
        compiler_params=_params("parallel"),
        name="mem_kv",
    )(mem2d, g.reshape(1, d), w, gk.reshape(1, -1))


def _xattn_kernel(q_ref, kn_ref, mv_ref, o_ref):
    xw = q_ref.shape[2]
    dh = xw // X_HEADS
    for h in range(X_HEADS):
        sl = slice(h * dh, (h + 1) * dh)
        logits = _nt_dot(q_ref[0, :, sl], kn_ref[0, :, sl]) * (dh ** -0.5)
        e = jnp.exp(logits - jnp.max(logits, axis=-1, keepdims=True))
        p = e / jnp.sum(e, axis=-1, keepdims=True)
        o_ref[0, :, sl] = _dot(p.astype(BF16), mv_ref[0, :, sl]).astype(o_ref.dtype)


def _cross_attention(qn, kn, mv, tm):
    b, s, w = qn.shape
    m = kn.shape[1]
    return pl.pallas_call(
        _xattn_kernel,
        grid=(b, s // tm),
        in_specs=[
            pl.BlockSpec((1, tm, w), lambda bi, i: (bi, i, 0)),
            pl.BlockSpec((1, m, w), lambda bi, i: (bi, 0, 0)),
            pl.BlockSpec((1, m, w), lambda bi, i: (bi, 0, 0)),
        ],
        out_specs=pl.BlockSpec((1, tm, w), lambda bi, i: (bi, i, 0)),
        out_shape=jax.ShapeDtypeStruct((b, s, w), BF16),
        compiler_params=_params("parallel", "parallel"),
        name="cross_attention",
    )(qn, kn, mv)


def _merge_kernel(x_ref, g_ref, ysb_ref, yml_ref, yx_ref, wg_ref, bg_ref, wsb_ref, wml_ref, wx_ref, wo_ref, o_ref):
    d = x_ref.shape[1]
    h = _rms(x_ref[...], g_ref[...]).astype(BF16)
    mixed = None
    for i, (y_ref, w_ref) in enumerate(((ysb_ref, wsb_ref), (yml_ref, wml_ref), (yx_ref, wx_ref))):
        gate = jax.nn.sigmoid(_nt_dot(h, wg_ref[i * d:(i + 1) * d, :]) + bg_ref[:, i * d:(i + 1) * d])
        term = gate * _dot(y_ref[...], w_ref[...])
        mixed = term if mixed is None else mixed + term
    o_ref[...] = x_ref[...] + _dot(mixed.astype(BF16), wo_ref[...])


def _merge(x2d, g, ysb, yml, yx, wg, bg, wsb, wml, wx, wo, tm):
    t, d = x2d.shape
    row = lambda: pl.BlockSpec((tm, d), lambda i: (i, 0))
    const = lambda shape: pl.BlockSpec(shape, lambda i: (0, 0), pipeline_mode=pl.Buffered(1))
    return pl.pallas_call(
        _merge_kernel,
        grid=(t // tm,),
        in_specs=[row(), const((1, d)), row(), row(), row(), const((N_BRANCH * d, d)), const((1, N_BRANCH * d)),
                  const((d, d)), const((d, d)), const((d, d)), const((d, d))],
        out_specs=row(),
        out_shape=jax.ShapeDtypeStruct((t, d), F32),
        compiler_params=_params("parallel"),
        name="merge_out",
    )(x2d, g.reshape(1, d), ysb, yml, yx, wg, bg.reshape(1, -1), wsb, wml, wx, wo)


def _mlp_kernel(x_ref, g_ref, w1_ref, w2_ref, o_ref, *, ff_chunk):
    x = x_ref[...]
    hb = _rms(x, g_ref[...]).astype(BF16)
    acc = x
    for c in range(w1_ref.shape[1] // ff_chunk):
        u = _dot(hb, w1_ref[:, c * ff_chunk:(c + 1) * ff_chunk])
        a = jnp.square(jnp.maximum(u, 0.0)).astype(BF16)
        acc = acc + _dot(a, w2_ref[c * ff_chunk:(c + 1) * ff_chunk, :])
    o_ref[...] = acc


def _mlp(x2d, g, w1, w2, tm, ff_chunk):
    t, d = x2d.shape
    ff = w1.shape[1]
    const = lambda shape: pl.BlockSpec(shape, lambda i: (0, 0), pipeline_mode=pl.Buffered(1))
    return pl.pallas_call(
        functools.partial(_mlp_kernel, ff_chunk=ff_chunk),
        grid=(t // tm,),
        in_specs=[pl.BlockSpec((tm, d), lambda i: (i, 0)), const((1, d)), const((d, ff)), const((ff, d))],
        out_specs=pl.BlockSpec((tm, d), lambda i: (i, 0)),
        out_shape=jax.ShapeDtypeStruct((t, d), F32),
        compiler_params=_params("parallel"),
        name="mlp",
    )(x2d, g.reshape(1, d), w1, w2)


def _layer(x, mem, g_mix, w_in, b_if, b_gate, conv_w, conv_b, ml_norm_g, g_mem, w_mem_kv, q_norm_g, k_norm_g,
           w_sb_proj, w_ml_proj, w_x_proj, w_out, g_mlp, w_ff1, w_ff2):
    b, s, d = x.shape
    m = mem.shape[1]
    t = b * s
    sb_w = w_sb_proj.shape[0]
    ml_w = w_ml_proj.shape[0]
    x_w = w_x_proj.shape[0]
    sb_heads = sb_w // SB_HD
    tm = min(512, t)

    sizes = (sb_w, sb_w, sb_w, ml_w, ml_w, ml_w, ml_w, ML_HEADS, ML_HEADS, x_w, N_BRANCH * d)
    offs = [0]
    for sz in sizes:
        offs.append(offs[-1] + sz)
    w_in_t = w_in.T.astype(BF16)
    col = lambda a, bnd: w_in_t[offs[a]:offs[bnd]]

    x2d = x.reshape(t, d)
    assert 2 * ML_HEADS == SUBLANES
    w_gif = jnp.pad(col(7, 9), ((0, LANES - 2 * ML_HEADS), (0, 0)))
    qkv, mlqk, mlv, mlo, gif, gif_t, qn = _inproj(x2d, g_mix, w_in_t, col(9, 10), w_gif, q_norm_g,
                                                  (3 * sb_w, 2 * ml_w, ml_w, ml_w, x_w, LANES), tm)

    y_sb = _sb_attention(qkv.reshape(b, s, 3 * sb_w), sb_heads).reshape(t, sb_w)

    y_ml = _mlstm(mlqk.reshape(b, s, 2 * ml_w), mlv.reshape(b, s, ml_w), mlo.reshape(b, s, ml_w),
                  gif.reshape(b, s, LANES), gif_t, b_if, conv_w, conv_b, ml_norm_g).reshape(t, ml_w)

    kn, mv = _mem_kv(mem.reshape(b * m, d), g_mem, w_mem_kv.astype(BF16), k_norm_g, min(512, b * m))
    y_x = _cross_attention(qn.reshape(b, s, x_w), kn.reshape(b, m, x_w), mv.reshape(b, m, x_w),
                           min(512, s)).reshape(t, x_w)

    x1 = _merge(x2d, g_mix, y_sb, y_ml, y_x, col(10, 11), b_gate, w_sb_proj.astype(BF16),
                w_ml_proj.astype(BF16), w_x_proj.astype(BF16), w_out.astype(BF16), tm)
    x2 = _mlp(x1, g_mlp, w_ff1.astype(BF16), w_ff2.astype(BF16), tm, min(1024, w_ff1.shape[1]))
    return x2.reshape(b, s, d)


def kernel(x, mem, g_mix, w_in, b_if, b_gate, conv_w, conv_b, ml_norm_g, g_mem, w_mem_kv, q_norm_g, k_norm_g,
           w_sb_proj, w_ml_proj, w_x_proj, w_out, g_mlp, w_ff1, w_ff2):
    for l in range(g_mix.shape[0]):
        x = _layer(x, mem, g_mix[l], w_in[l], b_if[l], b_gate[l], conv_w[l], conv_b[l], ml_norm_g[l], g_mem[l],
                   w_mem_kv[l], q_norm_g[l], k_norm_g[l], w_sb_proj[l], w_ml_proj[l], w_x_proj[l], w_out[l],
                   g_mlp[l], w_ff1[l], w_ff2[l])
    return x
```

```python
import functools

import jax
import jax.numpy as jnp
from jax import lax
from jax.experimental import pallas as pl
from jax.experimental.pallas import tpu as pltpu

EPS = 1e-6
LOG2E = 1.4426950408889634
SB_HD = 128
ML_HEADS = 4
X_HEADS = 4
CONV_W = 4
N_BRANCH = 3
SB_BLOCK = 256
SB_QBLOCKS = 8
SB_KBLOCKS = 2
INPROJ_COLS = 1024
ML_CHUNK = 512
CONV_PITCH = 4
SUBLANES = 8
LANES = 128
VMEM_LIMIT = 56 * 1024 * 1024

F32 = jnp.float32
BF16 = jnp.bfloat16


def _params(*sem):
    return pltpu.CompilerParams(dimension_semantics=sem, vmem_limit_bytes=VMEM_LIMIT)


def _nt_dot(a, b):
    return lax.dot_general(a, b, (((1,), (1,)), ((), ())), preferred_element_type=F32)


def _tn_dot(a, b):
    return lax.dot_general(a, b, (((0,), (0,)), ((), ())), preferred_element_type=F32)


def _dot(a, b):
    return jnp.dot(a, b, preferred_element_type=F32)


def _rms(x, g):
    return x * lax.rsqrt(jnp.mean(x * x, axis=-1, keepdims=True) + EPS) * g


def _inproj_kernel(x_ref, g_ref, w_ref, wxq_ref, wgif_ref, gq_ref, qkv_ref, mlqk_ref, mlv_ref, mlo_ref, gif_ref,
                   gift_ref, qn_ref, *, widths, q_scale):
    hb = _rms(x_ref[...], g_ref[...]).astype(BF16)
    sb3, ml2, mlv_w, mlo_w, xq_w, _ = widths
    step = INPROJ_COLS

    def proj(c0, c1, ref=w_ref):
        return _nt_dot(hb, ref[c0:c1, :])

    def emit(out_ref, base, width, scale_to=0):
        for c in range(0, width, step):
            y = proj(base + c, base + c + step)
            out_ref[:, c:c + step] = (y * q_scale if c < scale_to else y).astype(out_ref.dtype)

    emit(qkv_ref, 0, sb3, sb3 // 3)
    emit(mlqk_ref, sb3, ml2)
    emit(mlv_ref, sb3 + ml2, mlv_w)
    emit(mlo_ref, sb3 + ml2 + mlv_w, mlo_w)
    dh = xq_w // X_HEADS
    for h in range(X_HEADS):
        qn_ref[:, h * dh:(h + 1) * dh] = _rms(proj(h * dh, (h + 1) * dh, wxq_ref), gq_ref[...]).astype(BF16)
    gif_ref[...] = _nt_dot(hb, wgif_ref[...])
    gift_ref[...] = _nt_dot(wgif_ref[...], hb)[:gift_ref.shape[0]]


def _inproj(x2d, g, w_t, w_xq, w_gif, gq, widths, tm):
    t, d = x2d.shape
    sb3, ml2, mlv_w, mlo_w, xq_w, gif_w = widths
    assert sb3 % (3 * INPROJ_COLS) == 0
    row = lambda n: pl.BlockSpec((tm, n), lambda i: (i, 0))
    const = lambda shape: pl.BlockSpec(shape, lambda i: (0, 0), pipeline_mode=pl.Buffered(1))
    return pl.pallas_call(
        functools.partial(_inproj_kernel, widths=widths, q_scale=SB_HD ** -0.5 * LOG2E),
        grid=(t // tm,),
        in_specs=[row(d), const((1, d)), const((sb3 + ml2 + mlv_w + mlo_w, d)), const(w_xq.shape), const(w_gif.shape),
                  const((1, xq_w // X_HEADS))],
        out_specs=[row(sb3), row(ml2), row(mlv_w), row(mlo_w), row(gif_w),
                   pl.BlockSpec((SUBLANES, tm), lambda i: (0, i)), row(xq_w)],
        out_shape=[jax.ShapeDtypeStruct((t, sb3), BF16), jax.ShapeDtypeStruct((t, ml2), F32),
                   jax.ShapeDtypeStruct((t, mlv_w), BF16), jax.ShapeDtypeStruct((t, mlo_w), F32),
                   jax.ShapeDtypeStruct((t, gif_w), F32), jax.ShapeDtypeStruct((SUBLANES, t), F32),
                   jax.ShapeDtypeStruct((t, xq_w), BF16)],
        compiler_params=_params("parallel"),
        name="in_proj",
    )(x2d, g.reshape(1, d), w_t, w_xq, w_gif, gq.reshape(1, -1))


def _sb_kernel(q_ref, k_ref, v_ref, tri_ref, o_ref, acc_ref, carry_ref, z_ref, a_ref):
    blk = SB_BLOCK
    tq = blk * SB_QBLOCKS
    tk = blk * SB_KBLOCKS
    qi = pl.program_id(2)

    def key_start(g):
        return pl.multiple_of(g * tk, tk)

    def mask_top(x):
        causal = (lax.broadcasted_iota(jnp.int32, (blk, blk), 1) < lax.broadcasted_iota(jnp.int32, (blk, blk), 0))
        top = jnp.where(causal, x[:blk], 0.0)
        return top if x.shape[0] == blk else jnp.concatenate([top, x[blk:]], axis=0)

    def logits(g, r_base=0):
        return _nt_dot(q_ref[0, r_base:, :], k_ref[0, pl.ds(key_start(g), tk), :])

    def weights(z_all, r_base, diag):
        a_blocks = [None] * SB_KBLOCKS
        for j in reversed(range(SB_KBLOCKS)):
            r0 = j * blk if diag else 0
            z = z_all[r0:, j * blk:(j + 1) * blk]
            l = jnp.log(1.0 + jnp.exp2(-jnp.abs(z))) * LOG2E
            sp = jnp.maximum(z, 0.0) + l
            log_beta = z - sp
            if diag:
                sp = mask_top(sp)
            later = _dot(sp.astype(BF16), tri_ref[...])
            carry = carry_ref[r_base + r0:, :]
            a = jnp.exp2(log_beta - later - jnp.concatenate([carry] * (blk // LANES), axis=1))
            if diag:
                a = mask_top(a)
            carry_ref[r_base + r0:, :] = carry + (later[:, 0:1] + sp[:, 0:1])
            a = a.astype(BF16)
            a_blocks[j] = a if r0 == 0 else jnp.concatenate([jnp.zeros((r0, blk), BF16), a], axis=0)
        return jnp.concatenate(a_blocks, axis=1)

    def accumulate(a, g, r_base=0):
        acc_ref[r_base:, :] += _dot(a, v_ref[0, pl.ds(key_start(g), tk), :])

    carry_ref[...] = jnp.zeros_like(carry_ref)
    acc_ref[...] = jnp.zeros_like(acc_ref)
    n_diag = SB_QBLOCKS // SB_KBLOCKS
    assert n_diag * SB_KBLOCKS == SB_QBLOCKS and n_diag % 2 == 0
    n_full = qi * n_diag
    z_next = logits(n_full + n_diag - 1, (n_diag - 1) * tk)
    for d in reversed(range(n_diag)):
        z_cur = z_next
        if d > 0:
            z_next = logits(n_full + d - 1, (d - 1) * tk)
        else:
            z_ref[...] = logits(jnp.maximum(n_full - 1, 0))
        a = weights(z_cur, d * tk, True)
        if d > 0:
            accumulate(a, n_full + d, d * tk)
        else:
            a_ref[...] = a

    def body(i, c):
        ga = n_full - 1 - 2 * i
        gb = ga - 1
        z_b = logits(gb)
        accumulate(a_ref[...], ga + 1)
        a_a = weights(z_ref[...], 0, False)
        z_ref[...] = logits(jnp.maximum(ga - 2, 0))
        accumulate(a_a, ga)
        a_ref[...] = weights(z_b, 0, False)
        return c

    lax.fori_loop(0, n_full // 2, body, 0)
    accumulate(a_ref[...], 0)
    o_ref[0] = acc_ref[...].astype(o_ref.dtype)


def _sb_attention(qkv, n_heads):
    b, s, _ = qkv.shape
    blk = SB_BLOCK
    tq = blk * SB_QBLOCKS
    assert s % tq == 0
    tri = jnp.where(lax.broadcasted_iota(jnp.int32, (blk, blk), 0) > lax.broadcasted_iota(jnp.int32, (blk, blk), 1),
                    1.0, 0.0).astype(BF16)
    return pl.pallas_call(
        _sb_kernel,
        grid=(b, n_heads, s // tq),
        in_specs=[
            pl.BlockSpec((1, tq, SB_HD), lambda bi, h, i: (bi, i, h)),
            pl.BlockSpec((1, s, SB_HD), lambda bi, h, i: (bi, 0, n_heads + h)),
            pl.BlockSpec((1, s, SB_HD), lambda bi, h, i: (bi, 0, 2 * n_heads + h)),
            pl.BlockSpec((blk, blk), lambda bi, h, i: (0, 0)),
        ],
        out_specs=pl.BlockSpec((1, tq, SB_HD), lambda bi, h, i: (bi, i, h)),
        out_shape=jax.ShapeDtypeStruct((b, s, n_heads * SB_HD), BF16),
        scratch_shapes=[pltpu.VMEM((tq, SB_HD), F32), pltpu.VMEM((tq, LANES), F32),
                        pltpu.VMEM((tq, blk * SB_KBLOCKS), F32), pltpu.VMEM((tq, blk * SB_KBLOCKS), BF16)],
        compiler_params=_params("parallel", "parallel", "arbitrary"),
        name="sb_attention",
    )(qkv, qkv, qkv, tri)


def _silu(u):
    half = 0.5 * u
    return half + half * jnp.tanh(half)


def _conv_silu(raw_refs, w_ref, b_ref, lane0, halo_ref, out_ref, out_scale):
    P = CONV_PITCH
    rows = SUBLANES * P
    first = lax.broadcasted_iota(jnp.int32, (SUBLANES, LANES), 0) == 0
    for half, raw_ref in enumerate(raw_refs):
        lanes = slice(lane0 + half * LANES, lane0 + (half + 1) * LANES)
        w = [jnp.broadcast_to(w_ref[j:j + 1, lanes], (SUBLANES, LANES)) for j in range(CONV_W)]
        bias = jnp.broadcast_to(b_ref[:, lanes], (SUBLANES, LANES))
        tail = [halo_ref[half, m] for m in range(CONV_W - 1)]
        for blk in range(raw_ref.shape[1] // rows):
            base = blk * rows
            x = [raw_ref[0, pl.ds(base + i, SUBLANES, stride=P), :] for i in range(P)]
            moved = [pltpu.roll(x[P - (CONV_W - 1) + m], 1, axis=0) for m in range(CONV_W - 1)]
            virt = [jnp.where(first, tail[m], moved[m]) for m in range(CONV_W - 1)]
            ext = virt + x
            for i in range(P):
                u = bias
                for j in range(CONV_W):
                    u = u + w[j] * ext[i + j]
                y = _silu(u)
                out_ref[half, pl.ds(base + i, SUBLANES, stride=P), :] = y if out_scale == 1.0 else y * out_scale
            tail = moved
        for m in range(CONV_W - 1):
            halo_ref[half, m] = tail[m]


def _mlstm_kernel(*refs, k_scale, dh):
    nh = ML_HEADS
    L = ML_CHUNK
    bif_ref = refs[0]
    q_slabs, k_slabs = refs[1:1 + 2 * nh], refs[1 + 2 * nh:1 + 4 * nh]
    v_ref, o_ref, gcol_ref, grow_ref, cw_ref, cb_ref, ng_ref, y_ref = refs[1 + 4 * nh:9 + 4 * nh]
    qhalo, khalo, qs_ref, ks_ref, ct_ref, n_ref, m_ref = refs[9 + 4 * nh:]
    c = pl.program_id(1)
    heads = range(nh)

    @pl.when(c == 0)
    def _():
        qhalo[...] = jnp.zeros_like(qhalo)
        khalo[...] = jnp.zeros_like(khalo)
        ct_ref[...] = jnp.zeros_like(ct_ref)
        n_ref[...] = jnp.zeros_like(n_ref)
        m_ref[...] = jnp.zeros_like(m_ref)

    col = lambda h: slice(h * dh, (h + 1) * dh)
    for h in heads:
        _conv_silu(q_slabs[2 * h:2 * h + 2], cw_ref, cb_ref, h * dh, qhalo.at[h], qs_ref.at[h], 1.0)
        _conv_silu(k_slabs[2 * h:2 * h + 2], cw_ref, cb_ref, (nh + h) * dh, khalo.at[h], ks_ref.at[h], k_scale)

    t_idx = lax.broadcasted_iota(jnp.int32, (L, L), 0)
    s_idx = lax.broadcasted_iota(jnp.int32, (L, L), 1)
    tril = s_idx <= t_idx
    triu = t_idx <= s_idx

    gate = []
    for h in heads:
        b_i = bif_ref[h]
        b_f = bif_ref[nh + h]
        i_col = gcol_ref[0, :, h:h + 1] + b_i
        lf_col = jax.nn.log_sigmoid(gcol_ref[0, :, nh + h:nh + h + 1] + b_f)
        i_row = grow_ref[h:h + 1, :] + b_i
        lf_row = jax.nn.log_sigmoid(grow_ref[nh + h:nh + h + 1, :] + b_f)
        bcum_col = jnp.sum(jnp.where(tril, lf_row, 0.0), axis=1, keepdims=True)
        bcum_row = jnp.sum(jnp.where(triu, lf_col, 0.0), axis=0, keepdims=True)
        m_prev = m_ref[h]
        d = jnp.where(tril, bcum_col - bcum_row + i_row, -jnp.inf)
        m_inter = bcum_col + m_prev
        m_t = jnp.maximum(m_inter, jnp.max(d, axis=1, keepdims=True))
        b_end = jnp.sum(lf_row, axis=1, keepdims=True)
        g_col = b_end - bcum_col + i_col
        m_new = jnp.maximum(b_end + m_prev, jnp.max(g_col, axis=0, keepdims=True))
        gate.append(dict(w=jnp.exp(d - m_t), s_inter=jnp.exp(m_inter - m_t), floor=jnp.exp(-m_t),
                         decay=jnp.exp(b_end + m_prev - m_new), wk=jnp.exp(g_col - m_new), m_new=m_new))

    q = [jnp.concatenate([qs_ref[h, 0], qs_ref[h, 1]], axis=1) for h in heads]
    k = [jnp.concatenate([ks_ref[h, 0], ks_ref[h, 1]], axis=1) for h in heads]
    qb = [x.astype(BF16) for x in q]
    kb = [x.astype(BF16) for x in k]
    qk = [_nt_dot(qb[h], kb[h]) for h in heads]
    inter = [_dot(qb[h], ct_ref[h].astype(BF16)) for h in heads]

    for h in heads:
        g = gate[h]
        v = v_ref[0, :, col(h)]
        sc = qk[h] * g["w"]
        num = _dot(sc.astype(BF16), v) + g["s_inter"] * inter[h]
        den = jnp.sum(sc, axis=1, keepdims=True) + g["s_inter"] * jnp.sum(q[h] * n_ref[h], axis=1, keepdims=True)
        hh = num / jnp.maximum(jnp.abs(den), g["floor"])
        hn = _rms(hh, ng_ref[:, col(h)])
        y_ref[0, :, col(h)] = (hn * (0.5 + 0.5 * jnp.tanh(0.5 * o_ref[0, :, col(h)]))).astype(y_ref.dtype)

    for h in heads:
        g = gate[h]
        vw = (v_ref[0, :, col(h)].astype(F32) * g["wk"]).astype(BF16)
        ct_ref[h] = g["decay"] * ct_ref[h] + _tn_dot(kb[h], vw)
        n_ref[h] = g["decay"] * n_ref[h] + jnp.sum(k[h] * g["wk"], axis=0, keepdims=True)
        m_ref[h] = g["m_new"]


def _mlstm(mlqk, mlv, mlo, gif, gif_t, b_if, conv_w, conv_b, norm_g):
    b, s, w2 = mlqk.shape
    wd = w2 // 2
    nh = ML_HEADS
    dh = wd // nh
    assert dh == 2 * LANES
    L = ML_CHUNK
    nc = s // L
    slab = lambda j: pl.BlockSpec((1, L, LANES), lambda bi, c: (bi, c, j))
    full = lambda n: pl.BlockSpec((1, L, n), lambda bi, c: (bi, c, 0))
    const = lambda shape: pl.BlockSpec(shape, lambda bi, c: (0, 0))
    n_slabs = w2 // LANES
    halo = pltpu.VMEM((nh, 2, CONV_W - 1, SUBLANES, LANES), F32)
    slabs = pltpu.VMEM((nh, 2, L, LANES), F32)
    return pl.pallas_call(
        functools.partial(_mlstm_kernel, k_scale=dh ** -0.5, dh=dh),
        grid=(b, nc),
        in_specs=[pl.BlockSpec(memory_space=pltpu.SMEM)] + [slab(j) for j in range(n_slabs)] + [
            full(wd), full(wd), full(LANES),
            pl.BlockSpec((SUBLANES, L), lambda bi, c: (0, bi * nc + c)),
            const((CONV_W, w2)), const((1, w2)), const((1, wd)),
        ],
        out_specs=full(wd),
        out_shape=jax.ShapeDtypeStruct((b, s, wd), BF16),
        scratch_shapes=[halo, halo, slabs, slabs, pltpu.VMEM((nh, dh, dh), F32), pltpu.VMEM((nh, 1, dh), F32),
                        pltpu.VMEM((nh, 1, 1), F32)],
        compiler_params=_params("parallel", "arbitrary"),
        name="mlstm",
    )(b_if, *([mlqk] * n_slabs), mlv, mlo, gif, gif_t, conv_w, conv_b.reshape(1, w2), norm_g.reshape(1, wd))


def _mem_kv_kernel(m_ref, g_ref, w_ref, gk_ref, kn_ref, mv_ref):
    xw = kn_ref.shape[1]
    dh = xw // X_HEADS
    kv = _dot(_rms(m_ref[...], g_ref[...]).astype(BF16), w_ref[...])
    for h in range(X_HEADS):
        kn_ref[:, h * dh:(h + 1) * dh] = _rms(kv[:, h * dh:(h + 1) * dh], gk_ref[...]).astype(kn_ref.dtype)
    mv_ref[...] = kv[:, xw:].astype(mv_ref.dtype)


def _mem_kv(mem2d, g, w, gk, tm):
    t, d = mem2d.shape
    xw = w.shape[1] // 2
    const = lambda shape: pl.BlockSpec(shape, lambda i: (0, 0))
    return pl.pallas_call(
        _mem_kv_kernel,
        grid=(t // tm,),
        in_specs=[pl.BlockSpec((tm, d), lambda i: (i, 0)), const((1, d)), const(w.shape), const((1, xw // X_HEADS))],
        out_specs=[pl.BlockSpec((tm, xw), lambda i: (i, 0)), pl.BlockSpec((tm, xw), lambda i: (i, 0))],
        out_shape=[jax.ShapeDtypeStruct((t, xw), BF16), jax.ShapeDtypeStruct((t, xw), BF16)],
        compiler_params=_params("parallel"),
        name="mem_kv",
    )(mem2d, g.reshape(1, d), w, gk.reshape(1, -1))


def _xattn_kernel(q_ref, kn_ref, mv_ref, o_ref):
    xw = q_ref.shape[2]
    dh = xw // X_HEADS
    for h in range(X_HEADS):
        sl = slice(h * dh, (h + 1) * dh)
        logits = _nt_dot(q_ref[0, :, sl], kn_ref[0, :, sl]) * (dh ** -0.5)
        e = jnp.exp(logits - jnp.max(logits, axis=-1, keepdims=True))
        p = e / jnp.sum(e, axis=-1, keepdims=True)
        o_ref[0, :, sl] = _dot(p.astype(BF16), mv_ref[0, :, sl]).astype(o_ref.dtype)


def _cross_attention(qn, kn, mv, tm):
    b, s, w = qn.shape
    m = kn.shape[1]
    return pl.pallas_call(
        _xattn_kernel,
        grid=(b, s // tm),
        in_specs=[
            pl.BlockSpec((1, tm, w), lambda bi, i: (bi, i, 0)),
            pl.BlockSpec((1, m, w), lambda bi, i: (bi, 0, 0)),
            pl.BlockSpec((1, m, w), lambda bi, i: (bi, 0, 0)),
        ],
        out_specs=pl.BlockSpec((1, tm, w), lambda bi, i: (bi, i, 0)),
        out_shape=jax.ShapeDtypeStruct((b, s, w), BF16),
        compiler_params=_params("parallel", "parallel"),
        name="cross_attention",
    )(qn, kn, mv)


def _merge_kernel(x_ref, g_ref, ysb_ref, yml_ref, yx_ref, wg_ref, bg_ref, wsb_ref, wml_ref, wx_ref, wo_ref, o_ref):
    d = x_ref.shape[1]
    h = _rms(x_ref[...], g_ref[...]).astype(BF16)
    mixed = None
    for i, (y_ref, w_ref) in enumerate(((ysb_ref, wsb_ref), (yml_ref, wml_ref), (yx_ref, wx_ref))):
        gate = jax.nn.sigmoid(_nt_dot(h, wg_ref[i * d:(i + 1) * d, :]) + bg_ref[:, i * d:(i + 1) * d])
        term = gate * _dot(y_ref[...], w_ref[...])
        mixed = term if mixed is None else mixed + term
    o_ref[...] = x_ref[...] + _dot(mixed.astype(BF16), wo_ref[...])


def _merge(x2d, g, ysb, yml, yx, wg, bg, wsb, wml, wx, wo, tm):
    t, d = x2d.shape
    row = lambda: pl.BlockSpec((tm, d), lambda i: (i, 0))
    const = lambda shape: pl.BlockSpec(shape, lambda i: (0, 0), pipeline_mode=pl.Buffered(1))
    return pl.pallas_call(
        _merge_kernel,
        grid=(t // tm,),
        in_specs=[row(), const((1, d)), row(), row(), row(), const((N_BRANCH * d, d)), const((1, N_BRANCH * d)),
                  const((d, d)), const((d, d)), const((d, d)), const((d, d))],
        out_specs=row(),
        out_shape=jax.ShapeDtypeStruct((t, d), F32),
        compiler_params=_params("parallel"),
        name="merge_out",
    )(x2d, g.reshape(1, d), ysb, yml, yx, wg, bg.reshape(1, -1), wsb, wml, wx, wo)


def _mlp_kernel(x_ref, g_ref, w1_ref, w2_ref, o_ref, *, ff_chunk):
    x = x_ref[...]
    hb = _rms(x, g_ref[...]).astype(BF16)
    acc = x
    for c in range(w1_ref.shape[1] // ff_chunk):
        u = _dot(hb, w1_ref[:, c * ff_chunk:(c + 1) * ff_chunk])
        a = jnp.square(jnp.maximum(u, 0.0)).astype(BF16)
        acc = acc + _dot(a, w2_ref[c * ff_chunk:(c + 1) * ff_chunk, :])
    o_ref[...] = acc


def _mlp(x2d, g, w1, w2, tm, ff_chunk):
    t, d = x2d.shape
    ff = w1.shape[1]
    const = lambda shape: pl.BlockSpec(shape, lambda i: (0, 0), pipeline_mode=pl.Buffered(1))
    return pl.pallas_call(
        functools.partial(_mlp_kernel, ff_chunk=ff_chunk),
        grid=(t // tm,),
        in_specs=[pl.BlockSpec((tm, d), lambda i: (i, 0)), const((1, d)), const((d, ff)), const((ff, d))],
        out_specs=pl.BlockSpec((tm, d), lambda i: (i, 0)),
        out_shape=jax.ShapeDtypeStruct((t, d), F32),
        compiler_params=_params("parallel"),
        name="mlp",
    )(x2d, g.reshape(1, d), w1, w2)


def _layer(x, mem, g_mix, w_in, b_if, b_gate, conv_w, conv_b, ml_norm_g, g_mem, w_mem_kv, q_norm_g, k_norm_g,
           w_sb_proj, w_ml_proj, w_x_proj, w_out, g_mlp, w_ff1, w_ff2):
    b, s, d = x.shape
    m = mem.shape[1]
    t = b * s
    sb_w = w_sb_proj.shape[0]
    ml_w = w_ml_proj.shape[0]
    x_w = w_x_proj.shape[0]
    sb_heads = sb_w // SB_HD
    tm = min(512, t)

    sizes = (sb_w, sb_w, sb_w, ml_w, ml_w, ml_w, ml_w, ML_HEADS, ML_HEADS, x_w, N_BRANCH * d)
    offs = [0]
    for sz in sizes:
        offs.append(offs[-1] + sz)
    w_in_t = w_in.T.astype(BF16)
    col = lambda a, bnd: w_in_t[offs[a]:offs[bnd]]

    x2d = x.reshape(t, d)
    assert 2 * ML_HEADS == SUBLANES
    w_gif = jnp.pad(col(7, 9), ((0, LANES - 2 * ML_HEADS), (0, 0)))
    qkv, mlqk, mlv, mlo, gif, gif_t, qn = _inproj(x2d, g_mix, w_in_t, col(9, 10), w_gif, q_norm_g,
                                                  (3 * sb_w, 2 * ml_w, ml_w, ml_w, x_w, LANES), tm)

    y_sb = _sb_attention(qkv.reshape(b, s, 3 * sb_w), sb_heads).reshape(t, sb_w)

    y_ml = _mlstm(mlqk.reshape(b, s, 2 * ml_w), mlv.reshape(b, s, ml_w), mlo.reshape(b, s, ml_w),
                  gif.reshape(b, s, LANES), gif_t, b_if, conv_w, conv_b, ml_norm_g).reshape(t, ml_w)

    kn, mv = _mem_kv(mem.reshape(b * m, d), g_mem, w_mem_kv.astype(BF16), k_norm_g, min(512, b * m))
    y_x = _cross_attention(qn.reshape(b, s, x_w), kn.reshape(b, m, x_w), mv.reshape(b, m, x_w),
                           min(512, s)).reshape(t, x_w)

    x1 = _merge(x2d, g_mix, y_sb, y_ml, y_x, col(10, 11), b_gate, w_sb_proj.astype(BF16),
                w_ml_proj.astype(BF16), w_x_proj.astype(BF16), w_out.astype(BF16), tm)
    x2 = _mlp(x1, g_mlp, w_ff1.astype(BF16), w_ff2.astype(BF16), tm, min(1024, w_ff1.shape[1]))
    return x2.reshape(b, s, d)


def kernel(x, mem, g_mix, w_in, b_if, b_gate, conv_w, conv_b, ml_norm_g, g_mem, w_mem_kv, q_norm_g, k_norm_g,
           w_sb_proj, w_ml_proj, w_x_proj, w_out, g_mlp, w_ff1, w_ff2):
    for l in range(g_mix.shape[0]):
        x = _layer(x, mem, g_mix[l], w_in[l], b_if[l], b_gate[l], conv_w[l], conv_b[l], ml_norm_g[l], g_mem[l],
                   w_mem_kv[l], q_norm_g[l], k_norm_g[l], w_sb_proj[l], w_ml_proj[l], w_x_proj[l], w_out[l],
                   g_mlp[l], w_ff1[l], w_ff2[l])
    return x
```

```python
import functools

import jax
import jax.numpy as jnp
from jax import lax
from jax.experimental import pallas as pl
from jax.experimental.pallas import tpu as pltpu

EPS = 1e-6
LOG2E = 1.4426950408889634
SB_HD = 128
ML_HEADS = 4
X_HEADS = 4
CONV_W = 4
N_BRANCH = 3
SB_BLOCK = 256
SB_QBLOCKS = 8
SB_KBLOCKS = 2
XATTN_ROWS = 2048
INPROJ_COLS = 1024
ML_CHUNK = 512
CONV_PITCH = 4
SUBLANES = 8
LANES = 128
VMEM_LIMIT = 56 * 1024 * 1024

F32 = jnp.float32
BF16 = jnp.bfloat16


def _params(*sem):
    return pltpu.CompilerParams(dimension_semantics=sem, vmem_limit_bytes=VMEM_LIMIT)


def _nt_dot(a, b):
    return lax.dot_general(a, b, (((1,), (1,)), ((), ())), preferred_element_type=F32)


def _tn_dot(a, b):
    return lax.dot_general(a, b, (((0,), (0,)), ((), ())), preferred_element_type=F32)


def _dot(a, b):
    return jnp.dot(a, b, preferred_element_type=F32)


def _rms(x, g):
    return x * lax.rsqrt(jnp.mean(x * x, axis=-1, keepdims=True) + EPS) * g


def _inproj_kernel(x_ref, g_ref, w_ref, wxq_ref, wgif_ref, gq_ref, qkv_ref, mlqk_ref, mlv_ref, mlo_ref, gif_ref,
                   gift_ref, qn_ref, *, widths, q_scale):
    hb = _rms(x_ref[...], g_ref[...]).astype(BF16)
    sb3, ml2, mlv_w, mlo_w, xq_w, _ = widths
    step = INPROJ_COLS

    def proj(c0, c1, ref=w_ref):
        return _nt_dot(hb, ref[c0:c1, :])

    def emit(out_ref, base, width, scale_to=0):
        for c in range(0, width, step):
            y = proj(base + c, base + c + step)
            out_ref[:, c:c + step] = (y * q_scale if c < scale_to else y).astype(out_ref.dtype)

    emit(qkv_ref, 0, sb3, sb3 // 3)
    emit(mlqk_ref, sb3, ml2)
    emit(mlv_ref, sb3 + ml2, mlv_w)
    emit(mlo_ref, sb3 + ml2 + mlv_w, mlo_w)
    dh = xq_w // X_HEADS
    for h in range(X_HEADS):
        qn_ref[:, h * dh:(h + 1) * dh] = _rms(proj(h * dh, (h + 1) * dh, wxq_ref), gq_ref[...]).astype(BF16)
    gif_ref[...] = _nt_dot(hb, wgif_ref[...])
    gift_ref[...] = _nt_dot(wgif_ref[...], hb)[:gift_ref.shape[0]]


def _inproj(x2d, g, w_t, w_xq, w_gif, gq, widths, tm):
    t, d = x2d.shape
    sb3, ml2, mlv_w, mlo_w, xq_w, gif_w = widths
    assert sb3 % (3 * INPROJ_COLS) == 0
    row = lambda n: pl.BlockSpec((tm, n), lambda i: (i, 0))
    const = lambda shape: pl.BlockSpec(shape, lambda i: (0, 0), pipeline_mode=pl.Buffered(1))
    return pl.pallas_call(
        functools.partial(_inproj_kernel, widths=widths, q_scale=SB_HD ** -0.5 * LOG2E),
        grid=(t // tm,),
        in_specs=[row(d), const((1, d)), const((sb3 + ml2 + mlv_w + mlo_w, d)), const(w_xq.shape), const(w_gif.shape),
                  const((1, xq_w // X_HEADS))],
        out_specs=[row(sb3), row(ml2), row(mlv_w), row(mlo_w), row(gif_w),
                   pl.BlockSpec((SUBLANES, tm), lambda i: (0, i)), row(xq_w)],
        out_shape=[jax.ShapeDtypeStruct((t, sb3), BF16), jax.ShapeDtypeStruct((t, ml2), F32),
                   jax.ShapeDtypeStruct((t, mlv_w), BF16), jax.ShapeDtypeStruct((t, mlo_w), F32),
                   jax.ShapeDtypeStruct((t, gif_w), F32), jax.ShapeDtypeStruct((SUBLANES, t), F32),
                   jax.ShapeDtypeStruct((t, xq_w), BF16)],
        compiler_params=_params("parallel"),
        name="in_proj",
    )(x2d, g.reshape(1, d), w_t, w_xq, w_gif, gq.reshape(1, -1))


def _sb_kernel(q_ref, k_ref, v_ref, tri_ref, o_ref, acc_ref, carry_ref, z_ref, a_ref):
    blk = SB_BLOCK
    tq = blk * SB_QBLOCKS
    tk = blk * SB_KBLOCKS
    qi = pl.program_id(2)

    def key_start(g):
        return pl.multiple_of(g * tk, tk)

    def mask_top(x):
        causal = (lax.broadcasted_iota(jnp.int32, (blk, blk), 1) < lax.broadcasted_iota(jnp.int32, (blk, blk), 0))
        top = jnp.where(causal, x[:blk], 0.0)
        return top if x.shape[0] == blk else jnp.concatenate([top, x[blk:]], axis=0)

    def logits(g, r_base=0):
        return _nt_dot(q_ref[0, r_base:, :], k_ref[0, pl.ds(key_start(g), tk), :])

    def weights(z_all, r_base, diag):
        a_blocks = [None] * SB_KBLOCKS
        for j in reversed(range(SB_KBLOCKS)):
            r0 = j * blk if diag else 0
            z = z_all[r0:, j * blk:(j + 1) * blk]
            l = jnp.log(1.0 + jnp.exp2(-jnp.abs(z))) * LOG2E
            sp = jnp.maximum(z, 0.0) + l
            log_beta = z - sp
            if diag:
                sp = mask_top(sp)
            later = _dot(sp.astype(BF16), tri_ref[...])
            carry = carry_ref[r_base + r0:, :]
            a = jnp.exp2(log_beta - later - jnp.concatenate([carry] * (blk // LANES), axis=1))
            if diag:
                a = mask_top(a)
            carry_ref[r_base + r0:, :] = carry + (later[:, 0:1] + sp[:, 0:1])
            a = a.astype(BF16)
            a_blocks[j] = a if r0 == 0 else jnp.concatenate([jnp.zeros((r0, blk), BF16), a], axis=0)
        return jnp.concatenate(a_blocks, axis=1)

    def accumulate(a, g, r_base=0):
        acc_ref[r_base:, :] += _dot(a, v_ref[0, pl.ds(key_start(g), tk), :])

    carry_ref[...] = jnp.zeros_like(carry_ref)
    acc_ref[...] = jnp.zeros_like(acc_ref)
    n_diag = SB_QBLOCKS // SB_KBLOCKS
    assert n_diag * SB_KBLOCKS == SB_QBLOCKS and n_diag % 2 == 0
    n_full = qi * n_diag
    z_next = logits(n_full + n_diag - 1, (n_diag - 1) * tk)
    for d in reversed(range(n_diag)):
        z_cur = z_next
        if d > 0:
            z_next = logits(n_full + d - 1, (d - 1) * tk)
        else:
            z_ref[...] = logits(jnp.maximum(n_full - 1, 0))
        a = weights(z_cur, d * tk, True)
        if d > 0:
            accumulate(a, n_full + d, d * tk)
        else:
            a_ref[...] = a

    def body(i, c):
        ga = n_full - 1 - 2 * i
        gb = ga - 1
        z_b = logits(gb)
        accumulate(a_ref[...], ga + 1)
        a_a = weights(z_ref[...], 0, False)
        z_ref[...] = logits(jnp.maximum(ga - 2, 0))
        accumulate(a_a, ga)
        a_ref[...] = weights(z_b, 0, False)
        return c

    lax.fori_loop(0, n_full // 2, body, 0)
    accumulate(a_ref[...], 0)
    o_ref[0] = acc_ref[...].astype(o_ref.dtype)


def _sb_attention(qkv, n_heads):
    b, s, _ = qkv.shape
    blk = SB_BLOCK
    tq = blk * SB_QBLOCKS
    assert s % tq == 0
    tri = jnp.where(lax.broadcasted_iota(jnp.int32, (blk, blk), 0) > lax.broadcasted_iota(jnp.int32, (blk, blk), 1),
                    1.0, 0.0).astype(BF16)
    return pl.pallas_call(
        _sb_kernel,
        grid=(b, n_heads, s // tq),
        in_specs=[
            pl.BlockSpec((1, tq, SB_HD), lambda bi, h, i: (bi, i, h)),
            pl.BlockSpec((1, s, SB_HD), lambda bi, h, i: (bi, 0, n_heads + h)),
            pl.BlockSpec((1, s, SB_HD), lambda bi, h, i: (bi, 0, 2 * n_heads + h)),
            pl.BlockSpec((blk, blk), lambda bi, h, i: (0, 0)),
        ],
        out_specs=pl.BlockSpec((1, tq, SB_HD), lambda bi, h, i: (bi, i, h)),
        out_shape=jax.ShapeDtypeStruct((b, s, n_heads * SB_HD), BF16),
        scratch_shapes=[pltpu.VMEM((tq, SB_HD), F32), pltpu.VMEM((tq, LANES), F32),
                        pltpu.VMEM((tq, blk * SB_KBLOCKS), F32), pltpu.VMEM((tq, blk * SB_KBLOCKS), BF16)],
        compiler_params=_params("parallel", "parallel", "arbitrary"),
        name="sb_attention",
    )(qkv, qkv, qkv, tri)


def _silu(u):
    half = 0.5 * u
    return half + half * jnp.tanh(half)


def _conv_silu(raw_refs, w_ref, b_ref, lane0, halo_ref, out_ref, out_scale):
    P = CONV_PITCH
    rows = SUBLANES * P
    first = lax.broadcasted_iota(jnp.int32, (SUBLANES, LANES), 0) == 0
    for half, raw_ref in enumerate(raw_refs):
        lanes = slice(lane0 + half * LANES, lane0 + (half + 1) * LANES)
        w = [jnp.broadcast_to(w_ref[j:j + 1, lanes], (SUBLANES, LANES)) for j in range(CONV_W)]
        bias = jnp.broadcast_to(b_ref[:, lanes], (SUBLANES, LANES))
        tail = [halo_ref[half, m] for m in range(CONV_W - 1)]
        for blk in range(raw_ref.shape[1] // rows):
            base = blk * rows
            x = [raw_ref[0, pl.ds(base + i, SUBLANES, stride=P), :] for i in range(P)]
            moved = [pltpu.roll(x[P - (CONV_W - 1) + m], 1, axis=0) for m in range(CONV_W - 1)]
            virt = [jnp.where(first, tail[m], moved[m]) for m in range(CONV_W - 1)]
            ext = virt + x
            for i in range(P):
                u = bias
                for j in range(CONV_W):
                    u = u + w[j] * ext[i + j]
                y = _silu(u)
                out_ref[half, pl.ds(base + i, SUBLANES, stride=P), :] = y if out_scale == 1.0 else y * out_scale
            tail = moved
        for m in range(CONV_W - 1):
            halo_ref[half, m] = tail[m]


def _mlstm_kernel(*refs, k_scale, dh):
    nh = ML_HEADS
    L = ML_CHUNK
    bif_ref = refs[0]
    q_slabs, k_slabs = refs[1:1 + 2 * nh], refs[1 + 2 * nh:1 + 4 * nh]
    v_ref, o_ref, gcol_ref, grow_ref, cw_ref, cb_ref, ng_ref, y_ref = refs[1 + 4 * nh:9 + 4 * nh]
    qhalo, khalo, qs_ref, ks_ref, ct_ref, n_ref, m_ref = refs[9 + 4 * nh:]
    c = pl.program_id(1)
    heads = range(nh)

    @pl.when(c == 0)
    def _():
        qhalo[...] = jnp.zeros_like(qhalo)
        khalo[...] = jnp.zeros_like(khalo)
        ct_ref[...] = jnp.zeros_like(ct_ref)
        n_ref[...] = jnp.zeros_like(n_ref)
        m_ref[...] = jnp.zeros_like(m_ref)

    col = lambda h: slice(h * dh, (h + 1) * dh)
    for h in heads:
        _conv_silu(q_slabs[2 * h:2 * h + 2], cw_ref, cb_ref, h * dh, qhalo.at[h], qs_ref.at[h], 1.0)
        _conv_silu(k_slabs[2 * h:2 * h + 2], cw_ref, cb_ref, (nh + h) * dh, khalo.at[h], ks_ref.at[h], k_scale)

    t_idx = lax.broadcasted_iota(jnp.int32, (L, L), 0)
    s_idx = lax.broadcasted_iota(jnp.int32, (L, L), 1)
    tril = s_idx <= t_idx
    triu = t_idx <= s_idx

    gate = []
    for h in heads:
        b_i = bif_ref[h]
        b_f = bif_ref[nh + h]
        i_col = gcol_ref[0, :, h:h + 1] + b_i
        lf_col = jax.nn.log_sigmoid(gcol_ref[0, :, nh + h:nh + h + 1] + b_f)
        i_row = grow_ref[h:h + 1, :] + b_i
        lf_row = jax.nn.log_sigmoid(grow_ref[nh + h:nh + h + 1, :] + b_f)
        bcum_col = jnp.sum(jnp.where(tril, lf_row, 0.0), axis=1, keepdims=True)
        bcum_row = jnp.sum(jnp.where(triu, lf_col, 0.0), axis=0, keepdims=True)
        m_prev = m_ref[h]
        d = jnp.where(tril, bcum_col - bcum_row + i_row, -jnp.inf)
        m_inter = bcum_col + m_prev
        m_t = jnp.maximum(m_inter, jnp.max(d, axis=1, keepdims=True))
        b_end = jnp.sum(lf_row, axis=1, keepdims=True)
        g_col = b_end - bcum_col + i_col
        m_new = jnp.maximum(b_end + m_prev, jnp.max(g_col, axis=0, keepdims=True))
        gate.append(dict(w=jnp.exp(d - m_t), s_inter=jnp.exp(m_inter - m_t), floor=jnp.exp(-m_t),
                         decay=jnp.exp(b_end + m_prev - m_new), wk=jnp.exp(g_col - m_new), m_new=m_new))

    q = [jnp.concatenate([qs_ref[h, 0], qs_ref[h, 1]], axis=1) for h in heads]
    k = [jnp.concatenate([ks_ref[h, 0], ks_ref[h, 1]], axis=1) for h in heads]
    qb = [x.astype(BF16) for x in q]
    kb = [x.astype(BF16) for x in k]
    qk = [_nt_dot(qb[h], kb[h]) for h in heads]
    inter = [_dot(qb[h], ct_ref[h].astype(BF16)) for h in heads]

    for h in heads:
        g = gate[h]
        v = v_ref[0, :, col(h)]
        sc = qk[h] * g["w"]
        num = _dot(sc.astype(BF16), v) + g["s_inter"] * inter[h]
        den = jnp.sum(sc, axis=1, keepdims=True) + g["s_inter"] * jnp.sum(q[h] * n_ref[h], axis=1, keepdims=True)
        hh = num / jnp.maximum(jnp.abs(den), g["floor"])
        hn = _rms(hh, ng_ref[:, col(h)])
        y_ref[0, :, col(h)] = (hn * (0.5 + 0.5 * jnp.tanh(0.5 * o_ref[0, :, col(h)]))).astype(y_ref.dtype)

    for h in heads:
        g = gate[h]
        vw = (v_ref[0, :, col(h)].astype(F32) * g["wk"]).astype(BF16)
        ct_ref[h] = g["decay"] * ct_ref[h] + _tn_dot(kb[h], vw)
        n_ref[h] = g["decay"] * n_ref[h] + jnp.sum(k[h] * g["wk"], axis=0, keepdims=True)
        m_ref[h] = g["m_new"]


def _mlstm(mlqk, mlv, mlo, gif, gif_t, b_if, conv_w, conv_b, norm_g):
    b, s, w2 = mlqk.shape
    wd = w2 // 2
    nh = ML_HEADS
    dh = wd // nh
    assert dh == 2 * LANES
    L = ML_CHUNK
    nc = s // L
    slab = lambda j: pl.BlockSpec((1, L, LANES), lambda bi, c: (bi, c, j))
    full = lambda n: pl.BlockSpec((1, L, n), lambda bi, c: (bi, c, 0))
    const = lambda shape: pl.BlockSpec(shape, lambda bi, c: (0, 0))
    n_slabs = w2 // LANES
    halo = pltpu.VMEM((nh, 2, CONV_W - 1, SUBLANES, LANES), F32)
    slabs = pltpu.VMEM((nh, 2, L, LANES), F32)
    return pl.pallas_call(
        functools.partial(_mlstm_kernel, k_scale=dh ** -0.5, dh=dh),
        grid=(b, nc),
        in_specs=[pl.BlockSpec(memory_space=pltpu.SMEM)] + [slab(j) for j in range(n_slabs)] + [
            full(wd), full(wd), full(LANES),
            pl.BlockSpec((SUBLANES, L), lambda bi, c: (0, bi * nc + c)),
            const((CONV_W, w2)), const((1, w2)), const((1, wd)),
        ],
        out_specs=full(wd),
        out_shape=jax.ShapeDtypeStruct((b, s, wd), BF16),
        scratch_shapes=[halo, halo, slabs, slabs, pltpu.VMEM((nh, dh, dh), F32), pltpu.VMEM((nh, 1, dh), F32),
                        pltpu.VMEM((nh, 1, 1), F32)],
        compiler_params=_params("parallel", "arbitrary"),
        name="mlstm",
    )(b_if, *([mlqk] * n_slabs), mlv, mlo, gif, gif_t, conv_w, conv_b.reshape(1, w2), norm_g.reshape(1, wd))


def _mem_kv_kernel(m_ref, g_ref, w_ref, gk_ref, kn_ref, mv_ref):
    xw = kn_ref.shape[1]
    dh = xw // X_HEADS
    kv = _dot(_rms(m_ref[...], g_ref[...]).astype(BF16), w_ref[...])
    for h in range(X_HEADS):
        kn_ref[:, h * dh:(h + 1) * dh] = _rms(kv[:, h * dh:(h + 1) * dh], gk_ref[...]).astype(kn_ref.dtype)
    mv_ref[...] = kv[:, xw:].astype(mv_ref.dtype)


def _mem_kv(mem2d, g, w, gk, tm):
    t, d = mem2d.shape
    xw = w.shape[1] // 2
    const = lambda shape: pl.BlockSpec(shape, lambda i: (0, 0))
    return pl.pallas_call(
        _mem_kv_kernel,
        grid=(t // tm,),
        in_specs=[pl.BlockSpec((tm, d), lambda i: (i, 0)), const((1, d)), const(w.shape), const((1, xw // X_HEADS))],
        out_specs=[pl.BlockSpec((tm, xw), lambda i: (i, 0)), pl.BlockSpec((tm, xw), lambda i: (i, 0))],
        out_shape=[jax.ShapeDtypeStruct((t, xw), BF16), jax.ShapeDtypeStruct((t, xw), BF16)],
        compiler_params=_params("parallel"),
        name="mem_kv",
    )(mem2d, g.reshape(1, d), w, gk.reshape(1, -1))


def _xattn_kernel(q_ref, kn_ref, mv_ref, o_ref):
    xw = q_ref.shape[2]
    dh = xw // X_HEADS
    for h in range(X_HEADS):
        sl = slice(h * dh, (h + 1) * dh)
        logits = _nt_dot(q_ref[0, :, sl], kn_ref[0, :, sl]) * (dh ** -0.5)
        e = jnp.exp(logits - jnp.max(logits, axis=-1, keepdims=True))
        p = e / jnp.sum(e, axis=-1, keepdims=True)
        o_ref[0, :, sl] = _dot(p.astype(BF16), mv_ref[0, :, sl]).astype(o_ref.dtype)


def _cross_attention(qn, kn, mv, tm):
    b, s, w = qn.shape
    m = kn.shape[1]
    return pl.pallas_call(
        _xattn_kernel,
        grid=(b, s // tm),
        in_specs=[
            pl.BlockSpec((1, tm, w), lambda bi, i: (bi, i, 0)),
            pl.BlockSpec((1, m, w), lambda bi, i: (bi, 0, 0)),
            pl.BlockSpec((1, m, w), lambda bi, i: (bi, 0, 0)),
        ],
        out_specs=pl.BlockSpec((1, tm, w), lambda bi, i: (bi, i, 0)),
        out_shape=jax.ShapeDtypeStruct((b, s, w), BF16),
        compiler_params=_params("parallel", "parallel"),
        name="cross_attention",
    )(qn, kn, mv)


def _merge_kernel(x_ref, g_ref, ysb_ref, yml_ref, yx_ref, wg_ref, bg_ref, wsb_ref, wml_ref, wx_ref, wo_ref, o_ref):
    d = x_ref.shape[1]
    h = _rms(x_ref[...], g_ref[...]).astype(BF16)
    mixed = None
    for i, (y_ref, w_ref) in enumerate(((ysb_ref, wsb_ref), (yml_ref, wml_ref), (yx_ref, wx_ref))):
        gate = jax.nn.sigmoid(_nt_dot(h, wg_ref[i * d:(i + 1) * d, :]) + bg_ref[:, i * d:(i + 1) * d])
        term = gate * _dot(y_ref[...], w_ref[...])
        mixed = term if mixed is None else mixed + term
    o_ref[...] = x_ref[...] + _dot(mixed.astype(BF16), wo_ref[...])


def _merge(x2d, g, ysb, yml, yx, wg, bg, wsb, wml, wx, wo, tm):
    t, d = x2d.shape
    row = lambda: pl.BlockSpec((tm, d), lambda i: (i, 0))
    const = lambda shape: pl.BlockSpec(shape, lambda i: (0, 0), pipeline_mode=pl.Buffered(1))
    return pl.pallas_call(
        _merge_kernel,
        grid=(t // tm,),
        in_specs=[row(), const((1, d)), row(), row(), row(), const((N_BRANCH * d, d)), const((1, N_BRANCH * d)),
                  const((d, d)), const((d, d)), const((d, d)), const((d, d))],
        out_specs=row(),
        out_shape=jax.ShapeDtypeStruct((t, d), F32),
        compiler_params=_params("parallel"),
        name="merge_out",
    )(x2d, g.reshape(1, d), ysb, yml, yx, wg, bg.reshape(1, -1), wsb, wml, wx, wo)


def _mlp_kernel(x_ref, g_ref, w1_ref, w2_ref, o_ref, *, ff_chunk):
    x = x_ref[...]
    hb = _rms(x, g_ref[...]).astype(BF16)
    acc = x
    for c in range(w1_ref.shape[1] // ff_chunk):
        u = _dot(hb, w1_ref[:, c * ff_chunk:(c + 1) * ff_chunk])
        a = jnp.square(jnp.maximum(u, 0.0)).astype(BF16)
        acc = acc + _dot(a, w2_ref[c * ff_chunk:(c + 1) * ff_chunk, :])
    o_ref[...] = acc


def _mlp(x2d, g, w1, w2, tm, ff_chunk):
    t, d = x2d.shape
    ff = w1.shape[1]
    const = lambda shape: pl.BlockSpec(shape, lambda i: (0, 0), pipeline_mode=pl.Buffered(1))
    return pl.pallas_call(
        functools.partial(_mlp_kernel, ff_chunk=ff_chunk),
        grid=(t // tm,),
        in_specs=[pl.BlockSpec((tm, d), lambda i: (i, 0)), const((1, d)), const((d, ff)), const((ff, d))],
        out_specs=pl.BlockSpec((tm, d), lambda i: (i, 0)),
        out_shape=jax.ShapeDtypeStruct((t, d), F32),
        compiler_params=_params("parallel"),
        name="mlp",
    )(x2d, g.reshape(1, d), w1, w2)


def _layer(x, mem, g_mix, w_in, b_if, b_gate, conv_w, conv_b, ml_norm_g, g_mem, w_mem_kv, q_norm_g, k_norm_g,
           w_sb_proj, w_ml_proj, w_x_proj, w_out, g_mlp, w_ff1, w_ff2):
    b, s, d = x.shape
    m = mem.shape[1]
    t = b * s
    sb_w = w_sb_proj.shape[0]
    ml_w = w_ml_proj.shape[0]
    x_w = w_x_proj.shape[0]
    sb_heads = sb_w // SB_HD
    tm = min(512, t)

    sizes = (sb_w, sb_w, sb_w, ml_w, ml_w, ml_w, ml_w, ML_HEADS, ML_HEADS, x_w, N_BRANCH * d)
    offs = [0]
    for sz in sizes:
        offs.append(offs[-1] + sz)
    w_in_t = w_in.T.astype(BF16)
    col = lambda a, bnd: w_in_t[offs[a]:offs[bnd]]

    x2d = x.reshape(t, d)
    assert 2 * ML_HEADS == SUBLANES
    w_gif = jnp.pad(col(7, 9), ((0, LANES - 2 * ML_HEADS), (0, 0)))
    qkv, mlqk, mlv, mlo, gif, gif_t, qn = _inproj(x2d, g_mix, w_in_t, col(9, 10), w_gif, q_norm_g,
                                                  (3 * sb_w, 2 * ml_w, ml_w, ml_w, x_w, LANES), tm)

    y_sb = _sb_attention(qkv.reshape(b, s, 3 * sb_w), sb_heads).reshape(t, sb_w)

    y_ml = _mlstm(mlqk.reshape(b, s, 2 * ml_w), mlv.reshape(b, s, ml_w), mlo.reshape(b, s, ml_w),
                  gif.reshape(b, s, LANES), gif_t, b_if, conv_w, conv_b, ml_norm_g).reshape(t, ml_w)

    kn, mv = _mem_kv(mem.reshape(b * m, d), g_mem, w_mem_kv.astype(BF16), k_norm_g, min(512, b * m))
    y_x = _cross_attention(qn.reshape(b, s, x_w), kn.reshape(b, m, x_w), mv.reshape(b, m, x_w),
                           min(XATTN_ROWS, s)).reshape(t, x_w)

    x1 = _merge(x2d, g_mix, y_sb, y_ml, y_x, col(10, 11), b_gate, w_sb_proj.astype(BF16),
                w_ml_proj.astype(BF16), w_x_proj.astype(BF16), w_out.astype(BF16), tm)
    x2 = _mlp(x1, g_mlp, w_ff1.astype(BF16), w_ff2.astype(BF16), tm, min(1024, w_ff1.shape[1]))
    return x2.reshape(b, s, d)


def kernel(x, mem, g_mix, w_in, b_if, b_gate, conv_w, conv_b, ml_norm_g, g_mem, w_mem_kv, q_norm_g, k_norm_g,
           w_sb_proj, w_ml_proj, w_x_proj, w_out, g_mlp, w_ff1, w_ff2):
    for l in range(g_mix.shape[0]):
        x = _layer(x, mem, g_mix[l], w_in[l], b_if[l], b_gate[l], conv_w[l], conv_b[l], ml_norm_g[l], g_mem[l],
                   w_mem_kv[l], q_norm_g[l], k_norm_g[l], w_sb_proj[l], w_ml_proj[l], w_x_proj[l], w_out[l],
                   g_mlp[l], w_ff1[l], w_ff2[l])
    return x
```

```python
import functools

import jax
import jax.numpy as jnp
from jax import lax
from jax.experimental import pallas as pl
from jax.experimental.pallas import tpu as pltpu

EPS = 1e-6
LOG2E = 1.4426950408889634
SB_HD = 128
ML_HEADS = 4
X_HEADS = 4
CONV_W = 4
N_BRANCH = 3
SB_BLOCK = 256
SB_QBLOCKS = 8
SB_KBLOCKS = 2
XATTN_ROWS = 4096
MLP_ROWS = 1024
INPROJ_COLS = 1024
ML_CHUNK = 512
CONV_PITCH = 4
SUBLANES = 8
LANES = 128
VMEM_LIMIT = 56 * 1024 * 1024

F32 = jnp.float32
BF16 = jnp.bfloat16


def _params(*sem):
    return pltpu.CompilerParams(dimension_semantics=sem, vmem_limit_bytes=VMEM_LIMIT)


def _nt_dot(a, b):
    return lax.dot_general(a, b, (((1,), (1,)), ((), ())), preferred_element_type=F32)


def _tn_dot(a, b):
    return lax.dot_general(a, b, (((0,), (0,)), ((), ())), preferred_element_type=F32)


def _dot(a, b):
    return jnp.dot(a, b, preferred_element_type=F32)


def _rms(x, g):
    return x * lax.rsqrt(jnp.mean(x * x, axis=-1, keepdims=True) + EPS) * g


def _inproj_kernel(x_ref, g_ref, w_ref, wxq_ref, wgif_ref, gq_ref, qkv_ref, mlqk_ref, mlv_ref, mlo_ref, gif_ref,
                   gift_ref, qn_ref, *, widths, q_scale):
    hb = _rms(x_ref[...], g_ref[...]).astype(BF16)
    sb3, ml2, mlv_w, mlo_w, xq_w, _ = widths
    step = INPROJ_COLS

    def proj(c0, c1, ref=w_ref):
        return _nt_dot(hb, ref[c0:c1, :])

    def emit(out_ref, base, width, scale_to=0):
        for c in range(0, width, step):
            y = proj(base + c, base + c + step)
            out_ref[:, c:c + step] = (y * q_scale if c < scale_to else y).astype(out_ref.dtype)

    emit(qkv_ref, 0, sb3, sb3 // 3)
    emit(mlqk_ref, sb3, ml2)
    emit(mlv_ref, sb3 + ml2, mlv_w)
    emit(mlo_ref, sb3 + ml2 + mlv_w, mlo_w)
    dh = xq_w // X_HEADS
    for h in range(X_HEADS):
        qn_ref[:, h * dh:(h + 1) * dh] = _rms(proj(h * dh, (h + 1) * dh, wxq_ref), gq_ref[...]).astype(BF16)
    gif_ref[...] = _nt_dot(hb, wgif_ref[...])
    gift_ref[...] = _nt_dot(wgif_ref[...], hb)[:gift_ref.shape[0]]


def _inproj(x2d, g, w_t, w_xq, w_gif, gq, widths, tm):
    t, d = x2d.shape
    sb3, ml2, mlv_w, mlo_w, xq_w, gif_w = widths
    assert sb3 % (3 * INPROJ_COLS) == 0
    row = lambda n: pl.BlockSpec((tm, n), lambda i: (i, 0))
    const = lambda shape: pl.BlockSpec(shape, lambda i: (0, 0), pipeline_mode=pl.Buffered(1))
    return pl.pallas_call(
        functools.partial(_inproj_kernel, widths=widths, q_scale=SB_HD ** -0.5 * LOG2E),
        grid=(t // tm,),
        in_specs=[row(d), const((1, d)), const((sb3 + ml2 + mlv_w + mlo_w, d)), const(w_xq.shape), const(w_gif.shape),
                  const((1, xq_w // X_HEADS))],
        out_specs=[row(sb3), row(ml2), row(mlv_w), row(mlo_w), row(gif_w),
                   pl.BlockSpec((SUBLANES, tm), lambda i: (0, i)), row(xq_w)],
        out_shape=[jax.ShapeDtypeStruct((t, sb3), BF16), jax.ShapeDtypeStruct((t, ml2), F32),
                   jax.ShapeDtypeStruct((t, mlv_w), BF16), jax.ShapeDtypeStruct((t, mlo_w), F32),
                   jax.ShapeDtypeStruct((t, gif_w), F32), jax.ShapeDtypeStruct((SUBLANES, t), F32),
                   jax.ShapeDtypeStruct((t, xq_w), BF16)],
        compiler_params=_params("parallel"),
        name="in_proj",
    )(x2d, g.reshape(1, d), w_t, w_xq, w_gif, gq.reshape(1, -1))


def _sb_kernel(q_ref, k_ref, v_ref, tri_ref, o_ref, acc_ref, carry_ref, z_ref, a_ref):
    blk = SB_BLOCK
    tq = blk * SB_QBLOCKS
    tk = blk * SB_KBLOCKS
    qi = pl.program_id(2)

    def key_start(g):
        return pl.multiple_of(g * tk, tk)

    def mask_top(x):
        causal = (lax.broadcasted_iota(jnp.int32, (blk, blk), 1) < lax.broadcasted_iota(jnp.int32, (blk, blk), 0))
        top = jnp.where(causal, x[:blk], 0.0)
        return top if x.shape[0] == blk else jnp.concatenate([top, x[blk:]], axis=0)

    def logits(g, r_base=0):
        return _nt_dot(q_ref[0, r_base:, :], k_ref[0, pl.ds(key_start(g), tk), :])

    def weights(z_all, r_base, diag):
        a_blocks = [None] * SB_KBLOCKS
        for j in reversed(range(SB_KBLOCKS)):
            r0 = j * blk if diag else 0
            z = z_all[r0:, j * blk:(j + 1) * blk]
            l = jnp.log(1.0 + jnp.exp2(-jnp.abs(z))) * LOG2E
            sp = jnp.maximum(z, 0.0) + l
            log_beta = z - sp
            if diag:
                sp = mask_top(sp)
            later = _dot(sp.astype(BF16), tri_ref[...])
            carry = carry_ref[r_base + r0:, :]
            a = jnp.exp2(log_beta - later - jnp.concatenate([carry] * (blk // LANES), axis=1))
            if diag:
                a = mask_top(a)
            carry_ref[r_base + r0:, :] = carry + (later[:, 0:1] + sp[:, 0:1])
            a = a.astype(BF16)
            a_blocks[j] = a if r0 == 0 else jnp.concatenate([jnp.zeros((r0, blk), BF16), a], axis=0)
        return jnp.concatenate(a_blocks, axis=1)

    def accumulate(a, g, r_base=0):
        acc_ref[r_base:, :] += _dot(a, v_ref[0, pl.ds(key_start(g), tk), :])

    carry_ref[...] = jnp.zeros_like(carry_ref)
    acc_ref[...] = jnp.zeros_like(acc_ref)
    n_diag = SB_QBLOCKS // SB_KBLOCKS
    assert n_diag * SB_KBLOCKS == SB_QBLOCKS and n_diag % 2 == 0
    n_full = qi * n_diag
    z_next = logits(n_full + n_diag - 1, (n_diag - 1) * tk)
    for d in reversed(range(n_diag)):
        z_cur = z_next
        if d > 0:
            z_next = logits(n_full + d - 1, (d - 1) * tk)
        else:
            z_ref[...] = logits(jnp.maximum(n_full - 1, 0))
        a = weights(z_cur, d * tk, True)
        if d > 0:
            accumulate(a, n_full + d, d * tk)
        else:
            a_ref[...] = a

    def body(i, c):
        ga = n_full - 1 - 2 * i
        gb = ga - 1
        z_b = logits(gb)
        accumulate(a_ref[...], ga + 1)
        a_a = weights(z_ref[...], 0, False)
        z_ref[...] = logits(jnp.maximum(ga - 2, 0))
        accumulate(a_a, ga)
        a_ref[...] = weights(z_b, 0, False)
        return c

    lax.fori_loop(0, n_full // 2, body, 0)
    accumulate(a_ref[...], 0)
    o_ref[0] = acc_ref[...].astype(o_ref.dtype)


def _sb_attention(qkv, n_heads):
    b, s, _ = qkv.shape
    blk = SB_BLOCK
    tq = blk * SB_QBLOCKS
    assert s % tq == 0
    tri = jnp.where(lax.broadcasted_iota(jnp.int32, (blk, blk), 0) > lax.broadcasted_iota(jnp.int32, (blk, blk), 1),
                    1.0, 0.0).astype(BF16)
    return pl.pallas_call(
        _sb_kernel,
        grid=(b, n_heads, s // tq),
        in_specs=[
            pl.BlockSpec((1, tq, SB_HD), lambda bi, h, i: (bi, i, h)),
            pl.BlockSpec((1, s, SB_HD), lambda bi, h, i: (bi, 0, n_heads + h)),
            pl.BlockSpec((1, s, SB_HD), lambda bi, h, i: (bi, 0, 2 * n_heads + h)),
            pl.BlockSpec((blk, blk), lambda bi, h, i: (0, 0)),
        ],
        out_specs=pl.BlockSpec((1, tq, SB_HD), lambda bi, h, i: (bi, i, h)),
        out_shape=jax.ShapeDtypeStruct((b, s, n_heads * SB_HD), BF16),
        scratch_shapes=[pltpu.VMEM((tq, SB_HD), F32), pltpu.VMEM((tq, LANES), F32),
                        pltpu.VMEM((tq, blk * SB_KBLOCKS), F32), pltpu.VMEM((tq, blk * SB_KBLOCKS), BF16)],
        compiler_params=_params("parallel", "parallel", "arbitrary"),
        name="sb_attention",
    )(qkv, qkv, qkv, tri)


def _silu(u):
    half = 0.5 * u
    return half + half * jnp.tanh(half)


def _conv_silu(raw_refs, w_ref, b_ref, lane0, halo_ref, out_ref, out_scale):
    P = CONV_PITCH
    rows = SUBLANES * P
    first = lax.broadcasted_iota(jnp.int32, (SUBLANES, LANES), 0) == 0
    for half, raw_ref in enumerate(raw_refs):
        lanes = slice(lane0 + half * LANES, lane0 + (half + 1) * LANES)
        w = [jnp.broadcast_to(w_ref[j:j + 1, lanes], (SUBLANES, LANES)) for j in range(CONV_W)]
        bias = jnp.broadcast_to(b_ref[:, lanes], (SUBLANES, LANES))
        tail = [halo_ref[half, m] for m in range(CONV_W - 1)]
        for blk in range(raw_ref.shape[1] // rows):
            base = blk * rows
            x = [raw_ref[0, pl.ds(base + i, SUBLANES, stride=P), :] for i in range(P)]
            moved = [pltpu.roll(x[P - (CONV_W - 1) + m], 1, axis=0) for m in range(CONV_W - 1)]
            virt = [jnp.where(first, tail[m], moved[m]) for m in range(CONV_W - 1)]
            ext = virt + x
            for i in range(P):
                u = bias
                for j in range(CONV_W):
                    u = u + w[j] * ext[i + j]
                y = _silu(u)
                out_ref[half, pl.ds(base + i, SUBLANES, stride=P), :] = y if out_scale == 1.0 else y * out_scale
            tail = moved
        for m in range(CONV_W - 1):
            halo_ref[half, m] = tail[m]


def _mlstm_kernel(*refs, k_scale, dh):
    nh = ML_HEADS
    L = ML_CHUNK
    bif_ref = refs[0]
    q_slabs, k_slabs = refs[1:1 + 2 * nh], refs[1 + 2 * nh:1 + 4 * nh]
    v_ref, o_ref, gcol_ref, grow_ref, cw_ref, cb_ref, ng_ref, y_ref = refs[1 + 4 * nh:9 + 4 * nh]
    qhalo, khalo, qs_ref, ks_ref, ct_ref, n_ref, m_ref = refs[9 + 4 * nh:]
    c = pl.program_id(1)
    heads = range(nh)

    @pl.when(c == 0)
    def _():
        qhalo[...] = jnp.zeros_like(qhalo)
        khalo[...] = jnp.zeros_like(khalo)
        ct_ref[...] = jnp.zeros_like(ct_ref)
        n_ref[...] = jnp.zeros_like(n_ref)
        m_ref[...] = jnp.zeros_like(m_ref)

    col = lambda h: slice(h * dh, (h + 1) * dh)
    for h in heads:
        _conv_silu(q_slabs[2 * h:2 * h + 2], cw_ref, cb_ref, h * dh, qhalo.at[h], qs_ref.at[h], 1.0)
        _conv_silu(k_slabs[2 * h:2 * h + 2], cw_ref, cb_ref, (nh + h) * dh, khalo.at[h], ks_ref.at[h], k_scale)

    t_idx = lax.broadcasted_iota(jnp.int32, (L, L), 0)
    s_idx = lax.broadcasted_iota(jnp.int32, (L, L), 1)
    tril = s_idx <= t_idx
    triu = t_idx <= s_idx

    gate = []
    for h in heads:
        b_i = bif_ref[h]
        b_f = bif_ref[nh + h]
        i_col = gcol_ref[0, :, h:h + 1] + b_i
        lf_col = jax.nn.log_sigmoid(gcol_ref[0, :, nh + h:nh + h + 1] + b_f)
        i_row = grow_ref[h:h + 1, :] + b_i
        lf_row = jax.nn.log_sigmoid(grow_ref[nh + h:nh + h + 1, :] + b_f)
        bcum_col = jnp.sum(jnp.where(tril, lf_row, 0.0), axis=1, keepdims=True)
        bcum_row = jnp.sum(jnp.where(triu, lf_col, 0.0), axis=0, keepdims=True)
        m_prev = m_ref[h]
        d = jnp.where(tril, bcum_col - bcum_row + i_row, -jnp.inf)
        m_inter = bcum_col + m_prev
        m_t = jnp.maximum(m_inter, jnp.max(d, axis=1, keepdims=True))
        b_end = jnp.sum(lf_row, axis=1, keepdims=True)
        g_col = b_end - bcum_col + i_col
        m_new = jnp.maximum(b_end + m_prev, jnp.max(g_col, axis=0, keepdims=True))
        gate.append(dict(w=jnp.exp(d - m_t), s_inter=jnp.exp(m_inter - m_t), floor=jnp.exp(-m_t),
                         decay=jnp.exp(b_end + m_prev - m_new), wk=jnp.exp(g_col - m_new), m_new=m_new))

    q = [jnp.concatenate([qs_ref[h, 0], qs_ref[h, 1]], axis=1) for h in heads]
    k = [jnp.concatenate([ks_ref[h, 0], ks_ref[h, 1]], axis=1) for h in heads]
    qb = [x.astype(BF16) for x in q]
    kb = [x.astype(BF16) for x in k]
    qk = [_nt_dot(qb[h], kb[h]) for h in heads]
    inter = [_dot(qb[h], ct_ref[h].astype(BF16)) for h in heads]

    for h in heads:
        g = gate[h]
        v = v_ref[0, :, col(h)]
        sc = qk[h] * g["w"]
        num = _dot(sc.astype(BF16), v) + g["s_inter"] * inter[h]
        den = jnp.sum(sc, axis=1, keepdims=True) + g["s_inter"] * jnp.sum(q[h] * n_ref[h], axis=1, keepdims=True)
        hh = num / jnp.maximum(jnp.abs(den), g["floor"])
        hn = _rms(hh, ng_ref[:, col(h)])
        y_ref[0, :, col(h)] = (hn * (0.5 + 0.5 * jnp.tanh(0.5 * o_ref[0, :, col(h)]))).astype(y_ref.dtype)

    for h in heads:
        g = gate[h]
        vw = (v_ref[0, :, col(h)].astype(F32) * g["wk"]).astype(BF16)
        ct_ref[h] = g["decay"] * ct_ref[h] + _tn_dot(kb[h], vw)
        n_ref[h] = g["decay"] * n_ref[h] + jnp.sum(k[h] * g["wk"], axis=0, keepdims=True)
        m_ref[h] = g["m_new"]


def _mlstm(mlqk, mlv, mlo, gif, gif_t, b_if, conv_w, conv_b, norm_g):
    b, s, w2 = mlqk.shape
    wd = w2 // 2
    nh = ML_HEADS
    dh = wd // nh
    assert dh == 2 * LANES
    L = ML_CHUNK
    nc = s // L
    slab = lambda j: pl.BlockSpec((1, L, LANES), lambda bi, c: (bi, c, j))
    full = lambda n: pl.BlockSpec((1, L, n), lambda bi, c: (bi, c, 0))
    const = lambda shape: pl.BlockSpec(shape, lambda bi, c: (0, 0))
    n_slabs = w2 // LANES
    halo = pltpu.VMEM((nh, 2, CONV_W - 1, SUBLANES, LANES), F32)
    slabs = pltpu.VMEM((nh, 2, L, LANES), F32)
    return pl.pallas_call(
        functools.partial(_mlstm_kernel, k_scale=dh ** -0.5, dh=dh),
        grid=(b, nc),
        in_specs=[pl.BlockSpec(memory_space=pltpu.SMEM)] + [slab(j) for j in range(n_slabs)] + [
            full(wd), full(wd), full(LANES),
            pl.BlockSpec((SUBLANES, L), lambda bi, c: (0, bi * nc + c)),
            const((CONV_W, w2)), const((1, w2)), const((1, wd)),
        ],
        out_specs=full(wd),
        out_shape=jax.ShapeDtypeStruct((b, s, wd), BF16),
        scratch_shapes=[halo, halo, slabs, slabs, pltpu.VMEM((nh, dh, dh), F32), pltpu.VMEM((nh, 1, dh), F32),
                        pltpu.VMEM((nh, 1, 1), F32)],
        compiler_params=_params("parallel", "arbitrary"),
        name="mlstm",
    )(b_if, *([mlqk] * n_slabs), mlv, mlo, gif, gif_t, conv_w, conv_b.reshape(1, w2), norm_g.reshape(1, wd))


def _mem_kv_kernel(m_ref, g_ref, w_ref, gk_ref, kn_ref, mv_ref):
    xw = kn_ref.shape[1]
    dh = xw // X_HEADS
    kv = _dot(_rms(m_ref[...], g_ref[...]).astype(BF16), w_ref[...])
    for h in range(X_HEADS):
        kn_ref[:, h * dh:(h + 1) * dh] = _rms(kv[:, h * dh:(h + 1) * dh], gk_ref[...]).astype(kn_ref.dtype)
    mv_ref[...] = kv[:, xw:].astype(mv_ref.dtype)


def _mem_kv(mem2d, g, w, gk, tm):
    t, d = mem2d.shape
    xw = w.shape[1] // 2
    const = lambda shape: pl.BlockSpec(shape, lambda i: (0, 0))
    return pl.pallas_call(
        _mem_kv_kernel,
        grid=(t // tm,),
        in_specs=[pl.BlockSpec((tm, d), lambda i: (i, 0)), const((1, d)), const(w.shape), const((1, xw // X_HEADS))],
        out_specs=[pl.BlockSpec((tm, xw), lambda i: (i, 0)), pl.BlockSpec((tm, xw), lambda i: (i, 0))],
        out_shape=[jax.ShapeDtypeStruct((t, xw), BF16), jax.ShapeDtypeStruct((t, xw), BF16)],
        compiler_params=_params("parallel"),
        name="mem_kv",
    )(mem2d, g.reshape(1, d), w, gk.reshape(1, -1))


def _xattn_kernel(q_ref, kn_ref, mv_ref, o_ref):
    xw = q_ref.shape[2]
    dh = xw // X_HEADS
    for h in range(X_HEADS):
        sl = slice(h * dh, (h + 1) * dh)
        logits = _nt_dot(q_ref[0, :, sl], kn_ref[0, :, sl]) * (dh ** -0.5)
        e = jnp.exp(logits - jnp.max(logits, axis=-1, keepdims=True))
        p = e / jnp.sum(e, axis=-1, keepdims=True)
        o_ref[0, :, sl] = _dot(p.astype(BF16), mv_ref[0, :, sl]).astype(o_ref.dtype)


def _cross_attention(qn, kn, mv, tm):
    b, s, w = qn.shape
    m = kn.shape[1]
    return pl.pallas_call(
        _xattn_kernel,
        grid=(b, s // tm),
        in_specs=[
            pl.BlockSpec((1, tm, w), lambda bi, i: (bi, i, 0)),
            pl.BlockSpec((1, m, w), lambda bi, i: (bi, 0, 0)),
            pl.BlockSpec((1, m, w), lambda bi, i: (bi, 0, 0)),
        ],
        out_specs=pl.BlockSpec((1, tm, w), lambda bi, i: (bi, i, 0)),
        out_shape=jax.ShapeDtypeStruct((b, s, w), BF16),
        compiler_params=_params("parallel", "parallel"),
        name="cross_attention",
    )(qn, kn, mv)


def _merge_kernel(x_ref, g_ref, ysb_ref, yml_ref, yx_ref, wg_ref, bg_ref, wsb_ref, wml_ref, wx_ref, wo_ref, o_ref):
    d = x_ref.shape[1]
    h = _rms(x_ref[...], g_ref[...]).astype(BF16)
    mixed = None
    for i, (y_ref, w_ref) in enumerate(((ysb_ref, wsb_ref), (yml_ref, wml_ref), (yx_ref, wx_ref))):
        gate = jax.nn.sigmoid(_nt_dot(h, wg_ref[i * d:(i + 1) * d, :]) + bg_ref[:, i * d:(i + 1) * d])
        term = gate * _dot(y_ref[...], w_ref[...])
        mixed = term if mixed is None else mixed + term
    o_ref[...] = x_ref[...] + _dot(mixed.astype(BF16), wo_ref[...])


def _merge(x2d, g, ysb, yml, yx, wg, bg, wsb, wml, wx, wo, tm):
    t, d = x2d.shape
    row = lambda: pl.BlockSpec((tm, d), lambda i: (i, 0))
    const = lambda shape: pl.BlockSpec(shape, lambda i: (0, 0), pipeline_mode=pl.Buffered(1))
    return pl.pallas_call(
        _merge_kernel,
        grid=(t // tm,),
        in_specs=[row(), const((1, d)), row(), row(), row(), const((N_BRANCH * d, d)), const((1, N_BRANCH * d)),
                  const((d, d)), const((d, d)), const((d, d)), const((d, d))],
        out_specs=row(),
        out_shape=jax.ShapeDtypeStruct((t, d), F32),
        compiler_params=_params("parallel"),
        name="merge_out",
    )(x2d, g.reshape(1, d), ysb, yml, yx, wg, bg.reshape(1, -1), wsb, wml, wx, wo)


def _mlp_kernel(x_ref, g_ref, w1_ref, w2_ref, o_ref, *, ff_chunk):
    x = x_ref[...]
    hb = _rms(x, g_ref[...]).astype(BF16)
    acc = x
    for c in range(w1_ref.shape[1] // ff_chunk):
        u = _dot(hb, w1_ref[:, c * ff_chunk:(c + 1) * ff_chunk])
        a = jnp.square(jnp.maximum(u, 0.0)).astype(BF16)
        acc = acc + _dot(a, w2_ref[c * ff_chunk:(c + 1) * ff_chunk, :])
    o_ref[...] = acc


def _mlp(x2d, g, w1, w2, tm, ff_chunk):
    t, d = x2d.shape
    ff = w1.shape[1]
    const = lambda shape: pl.BlockSpec(shape, lambda i: (0, 0), pipeline_mode=pl.Buffered(1))
    return pl.pallas_call(
        functools.partial(_mlp_kernel, ff_chunk=ff_chunk),
        grid=(t // tm,),
        in_specs=[pl.BlockSpec((tm, d), lambda i: (i, 0)), const((1, d)), const((d, ff)), const((ff, d))],
        out_specs=pl.BlockSpec((tm, d), lambda i: (i, 0)),
        out_shape=jax.ShapeDtypeStruct((t, d), F32),
        compiler_params=_params("parallel"),
        name="mlp",
    )(x2d, g.reshape(1, d), w1, w2)


def _layer(x, mem, g_mix, w_in, b_if, b_gate, conv_w, conv_b, ml_norm_g, g_mem, w_mem_kv, q_norm_g, k_norm_g,
           w_sb_proj, w_ml_proj, w_x_proj, w_out, g_mlp, w_ff1, w_ff2):
    b, s, d = x.shape
    m = mem.shape[1]
    t = b * s
    sb_w = w_sb_proj.shape[0]
    ml_w = w_ml_proj.shape[0]
    x_w = w_x_proj.shape[0]
    sb_heads = sb_w // SB_HD
    tm = min(512, t)

    sizes = (sb_w, sb_w, sb_w, ml_w, ml_w, ml_w, ml_w, ML_HEADS, ML_HEADS, x_w, N_BRANCH * d)
    offs = [0]
    for sz in sizes:
        offs.append(offs[-1] + sz)
    w_in_t = w_in.T.astype(BF16)
    col = lambda a, bnd: w_in_t[offs[a]:offs[bnd]]

    x2d = x.reshape(t, d)
    assert 2 * ML_HEADS == SUBLANES
    w_gif = jnp.pad(col(7, 9), ((0, LANES - 2 * ML_HEADS), (0, 0)))
    qkv, mlqk, mlv, mlo, gif, gif_t, qn = _inproj(x2d, g_mix, w_in_t, col(9, 10), w_gif, q_norm_g,
                                                  (3 * sb_w, 2 * ml_w, ml_w, ml_w, x_w, LANES), tm)

    y_sb = _sb_attention(qkv.reshape(b, s, 3 * sb_w), sb_heads).reshape(t, sb_w)

    y_ml = _mlstm(mlqk.reshape(b, s, 2 * ml_w), mlv.reshape(b, s, ml_w), mlo.reshape(b, s, ml_w),
                  gif.reshape(b, s, LANES), gif_t, b_if, conv_w, conv_b, ml_norm_g).reshape(t, ml_w)

    kn, mv = _mem_kv(mem.reshape(b * m, d), g_mem, w_mem_kv.astype(BF16), k_norm_g, min(512, b * m))
    y_x = _cross_attention(qn.reshape(b, s, x_w), kn.reshape(b, m, x_w), mv.reshape(b, m, x_w),
                           min(XATTN_ROWS, s)).reshape(t, x_w)

    x1 = _merge(x2d, g_mix, y_sb, y_ml, y_x, col(10, 11), b_gate, w_sb_proj.astype(BF16),
                w_ml_proj.astype(BF16), w_x_proj.astype(BF16), w_out.astype(BF16), tm)
    x2 = _mlp(x1, g_mlp, w_ff1.astype(BF16), w_ff2.astype(BF16), min(MLP_ROWS, t), min(1024, w_ff1.shape[1]))
    return x2.reshape(b, s, d)


def kernel(x, mem, g_mix, w_in, b_if, b_gate, conv_w, conv_b, ml_norm_g, g_mem, w_mem_kv, q_norm_g, k_norm_g,
           w_sb_proj, w_ml_proj, w_x_proj, w_out, g_mlp, w_ff1, w_ff2):
    for l in range(g_mix.shape[0]):
        x = _layer(x, mem, g_mix[l], w_in[l], b_if[l], b_gate[l], conv_w[l], conv_b[l], ml_norm_g[l], g_mem[l],
                   w_mem_kv[l], q_norm_g[l], k_norm_g[l], w_sb_proj[l], w_ml_proj[l], w_x_proj[l], w_out[l],
                   g_mlp[l], w_ff1[l], w_ff2[l])
    return x
```
